```python
import math, functools
import jax
import jax.numpy as jnp
from jax import lax
import numpy as np

D_MODEL = 2048
BATCH = 2
SEQ = 4096
DEPTH = 2
DEC_BATCH = 128
DEC_SEQ = 8
PAST_LEN = 16384
PAGE_SIZE = 128

POOL_WINDOWS = (2, 4, 8, 16)
POOL_WIDTH = D_MODEL // 4
POOL_GROUP = POOL_WIDTH // len(POOL_WINDOWS)
POOL_STATE = max(POOL_WINDOWS) - 1
MLA_HEADS = 6
MLA_NOPE = 64
MLA_ROPE = 32
MLA_QK = MLA_NOPE + MLA_ROPE
MLA_V = 128
MLA_WIDTH = MLA_HEADS * MLA_V
Q_LORA = 384
KV_LORA = 128
MLA_CACHE = KV_LORA + MLA_ROPE
ROPE_THETA = 10000.0
NSA_HEADS = 12
NSA_DIM = 64
NSA_GROUPS = 2
NSA_REP = NSA_HEADS // NSA_GROUPS
NSA_WIDTH = NSA_HEADS * NSA_DIM
NSA_KV = NSA_GROUPS * NSA_DIM
NSA_BLOCK = 64
NSA_TOPN = 16
WINDOW = 512
MIX_WIDTH = POOL_WIDTH + MLA_WIDTH + NSA_WIDTH
IN_SPLITS = (POOL_WIDTH, Q_LORA, KV_LORA, MLA_ROPE, NSA_WIDTH) + (NSA_KV,) * 6 + (3 * NSA_HEADS,)
IN_COLS = sum(IN_SPLITS)
N_BUCKETS = 32
MAX_DISTANCE = 1024
D_FF = -(-8 * D_MODEL // (3 * 256)) * 256
Q_BLOCK = 128
EPS = 1e-6
NEG = -1e30

kernel_name = 'hybrid_pool_mla_nsa_decoder_step'


def _rmsnorm(x, g):
    xf = x.astype(jnp.float32)
    y = xf * lax.rsqrt(jnp.mean(xf * xf, axis=-1, keepdims=True) + EPS)
    return (y * g.astype(jnp.float32)).astype(x.dtype)


def _masked_softmax(s, mask):
    s = jnp.where(mask, s.astype(jnp.float32), NEG)
    p = jnp.exp(s - jnp.max(s, axis=-1, keepdims=True)) * mask
    return p / jnp.maximum(jnp.sum(p, axis=-1, keepdims=True), 1e-30)


def _rope(x, pos):
    half = x.shape[-1] // 2
    inv = ROPE_THETA ** (-jnp.arange(half, dtype=jnp.float32) / half)
    ang = pos.astype(jnp.float32)[:, None] * inv
    cos, sin = jnp.cos(ang)[:, None, :], jnp.sin(ang)[:, None, :]
    xf = x.astype(jnp.float32)
    x1, x2 = xf[..., :half], xf[..., half:]
    return jnp.concatenate([x1 * cos - x2 * sin, x1 * sin + x2 * cos], axis=-1).astype(x.dtype)


def _rel_bucket(dist):
    n = jnp.maximum(dist, 0)
    exact = N_BUCKETS // 2
    nf = jnp.maximum(n, exact).astype(jnp.float32)
    large = exact + (jnp.log(nf / exact) / math.log(MAX_DISTANCE / exact) * (N_BUCKETS - exact)).astype(jnp.int32)
    return jnp.where(n < exact, n, jnp.minimum(large, N_BUCKETS - 1))


def _bias_shared(rel_bias, dist):
    T, L = dist.shape
    b = rel_bias[_rel_bucket(dist)]
    return b.reshape(T, L, NSA_GROUPS, NSA_REP).transpose(0, 2, 3, 1)


def _bias_per_group(rel_bias, dist):
    bt = rel_bias.reshape(N_BUCKETS, NSA_GROUPS, NSA_REP)
    gi = jnp.arange(NSA_GROUPS)[None, :, None]
    return bt[_rel_bucket(dist), gi].transpose(0, 1, 3, 2)


def _split_in(z):
    cuts = [int(c) for c in np.cumsum(IN_SPLITS)[:-1]]
    return jnp.split(z, cuts, axis=-1)


def _mla_q(q_lat, p, pos):
    q = _rmsnorm(q_lat, p['mla_q_norm']) @ p['w_uq']
    q = _rmsnorm(q.reshape(q.shape[:-1] + (MLA_HEADS, MLA_QK)), p['mla_q_gain'])
    return jnp.concatenate([q[..., :MLA_NOPE], _rope(q[..., MLA_NOPE:], pos)], axis=-1)


def _mla_kv(rows, p, pos):
    kv = (rows[..., :KV_LORA] @ p['w_ukv']).reshape(rows.shape[:-1] + (MLA_HEADS, MLA_NOPE + MLA_V))
    k_pe = jnp.broadcast_to(rows[..., None, KV_LORA:], kv.shape[:-1] + (MLA_ROPE,))
    k = _rmsnorm(jnp.concatenate([kv[..., :MLA_NOPE], k_pe], axis=-1), p['mla_k_gain'])
    k = jnp.concatenate([k[..., :MLA_NOPE], _rope(k[..., MLA_NOPE:], pos)], axis=-1)
    return k, kv[..., MLA_NOPE:]


def _mha(q, k, v, mask):
    s = jnp.einsum('thd,lhd->htl', q, k).astype(jnp.float32) * q.shape[-1] ** -0.5
    pr = _masked_softmax(s, mask[None])
    return jnp.einsum('htl,lhd->thd', pr.astype(v.dtype), v)


def _mla_prompt_seq(q, k, v):
    T = q.shape[0]
    k_pos = jnp.arange(T)

    def q_block(q0):
        qb = lax.dynamic_slice_in_dim(q, q0, Q_BLOCK)
        return _mha(qb, k, v, k_pos[None, :] <= (q0 + jnp.arange(Q_BLOCK))[:, None])

    return lax.map(q_block, jnp.arange(0, T, Q_BLOCK)).reshape(T, MLA_HEADS, MLA_V)


def _mla_sample_seq(args, cache, p):
    q, new_rows, pages = args
    S = q.shape[0]
    past = cache[pages].reshape(-1, MLA_CACHE)
    n_past = past.shape[0]
    k_pos = jnp.arange(n_past + S)
    k, v = _mla_kv(jnp.concatenate([past, new_rows], axis=0), p, k_pos)
    q_pos = n_past + jnp.arange(S)
    return _mha(q, k, v, k_pos[None, :] <= q_pos[:, None])


def _nsa_cmp(q, kc, vc, q_pos, blk_end, rel_bias):
    dist = q_pos[:, None] - blk_end[None, :]
    s = jnp.einsum('tgrd,ngd->tgrn', q, kc).astype(jnp.float32) * NSA_DIM ** -0.5 + _bias_shared(rel_bias, dist)
    pr = _masked_softmax(s, (dist >= 0)[:, None, None, :])
    o = jnp.einsum('tgrn,ngd->tgrd', pr.astype(vc.dtype), vc)
    return o, jnp.sum(pr, axis=2)


def _nsa_select(imp, q_pos, n_blocks):
    j = jnp.arange(n_blocks)
    cur = (q_pos // NSA_BLOCK)[:, None]
    forced = ((j == cur) | (j == 0))[:, None, :]
    allowed = (j <= cur)[:, None, :]
    score = jnp.where(allowed, jnp.where(forced, -NEG, imp.astype(jnp.float32)), NEG)
    _, idx = lax.top_k(score, min(NSA_TOPN, n_blocks))
    return idx


def _nsa_sel(q, k_rows, v_rows, idx, q_pos, rel_bias):
    T, G, K = idx.shape
    m = K * NSA_BLOCK
    k_pos = (idx[..., None] * NSA_BLOCK + jnp.arange(NSA_BLOCK)).reshape(T, G, m)
    dist = q_pos[:, None, None] - k_pos
    kr = k_rows.reshape(T, G, m, NSA_DIM)
    vr = v_rows.reshape(T, G, m, NSA_DIM)
    s = jnp.einsum('tgrd,tgmd->tgrm', q, kr).astype(jnp.float32) * NSA_DIM ** -0.5 + _bias_per_group(rel_bias, dist)
    pr = _masked_softmax(s, (dist >= 0)[:, :, None, :])
    return jnp.einsum('tgrm,tgmd->tgrd', pr.astype(vr.dtype), vr)


def _nsa_win(q, k, v, q_pos, k_pos, rel_bias):
    dist = q_pos[:, None] - k_pos[None, :]
    mask = (dist >= 0) & (dist < WINDOW) & (k_pos >= 0)[None, :]
    s = jnp.einsum('tgrd,lgd->tgrl', q, k).astype(jnp.float32) * NSA_DIM ** -0.5 + _bias_shared(rel_bias, dist)
    pr = _masked_softmax(s, mask[:, None, None, :])
    return jnp.einsum('tgrl,lgd->tgrd', pr.astype(v.dtype), v)


def _nsa_combine(gates, o_cmp, o_sel, o_win):
    g = gates[..., None]
    return g[:, 0] * o_cmp + g[:, 1] * o_sel + g[:, 2] * o_win


def _nsa_prompt_seq(q, k_cmp, v_cmp, k_sel, v_sel, k_win, v_win, gates, kc_gain, rel_bias):
    T = q.shape[0]
    nb = T // NSA_BLOCK
    blocks = lambda a: a.reshape(nb, NSA_BLOCK, NSA_GROUPS, NSA_DIM)
    pos = jnp.arange(T)
    kc = _rmsnorm(jnp.mean(blocks(k_cmp), axis=1), kc_gain)
    vc = jnp.mean(blocks(v_cmp), axis=1)
    o_cmp, imp = _nsa_cmp(q, kc, vc, pos, jnp.arange(nb) * NSA_BLOCK + NSA_BLOCK - 1, rel_bias)
    idx = _nsa_select(imp, pos, nb)
    ks_b = blocks(k_sel).transpose(2, 0, 1, 3)
    vs_b = blocks(v_sel).transpose(2, 0, 1, 3)
    gi = jnp.arange(NSA_GROUPS)[None, :, None]
    kw_pad = jnp.pad(k_win, ((WINDOW, 0), (0, 0), (0, 0)))
    vw_pad = jnp.pad(v_win, ((WINDOW, 0), (0, 0), (0, 0)))

    def q_block(q0):
        qb = lax.dynamic_slice_in_dim(q, q0, Q_BLOCK)
        pb = q0 + jnp.arange(Q_BLOCK)
        ib = lax.dynamic_slice_in_dim(idx, q0, Q_BLOCK)
        o_sel = _nsa_sel(qb, ks_b[gi, ib], vs_b[gi, ib], ib, pb, rel_bias)
        k_pos = q0 - WINDOW + jnp.arange(WINDOW + Q_BLOCK)
        o_win = _nsa_win(qb, lax.dynamic_slice_in_dim(kw_pad, q0, WINDOW + Q_BLOCK),
                         lax.dynamic_slice_in_dim(vw_pad, q0, WINDOW + Q_BLOCK), pb, k_pos, rel_bias)
        return o_sel, o_win

    o_sel, o_win = lax.map(q_block, jnp.arange(0, T, Q_BLOCK))
    shape = (T, NSA_GROUPS, NSA_REP, NSA_DIM)
    return _nsa_combine(gates, o_cmp, o_sel.reshape(shape), o_win.reshape(shape))


def _nsa_sample_seq(args, cache_cmp, cache_sel, kc_gain, rel_bias):
    q, k_cmp, v_cmp, k_sel, v_sel, k_win, v_win, gates, pages, win_buf = args
    S = q.shape[0]
    n_past = pages.shape[0] * PAGE_SIZE
    nbp = n_past // NSA_BLOCK
    nbn = -(-S // NSA_BLOCK)
    per_page = PAGE_SIZE // NSA_BLOCK
    pad = nbn * NSA_BLOCK - S
    new_blocks = lambda a: jnp.pad(a, ((0, pad), (0, 0), (0, 0))).reshape(nbn, NSA_BLOCK, NSA_GROUPS, NSA_DIM)
    past_cmp = jnp.mean(cache_cmp[pages].reshape(nbp, NSA_BLOCK, 2, NSA_GROUPS, NSA_DIM), axis=1)
    kc = _rmsnorm(jnp.concatenate([past_cmp[:, 0], jnp.mean(new_blocks(k_cmp), axis=1)], axis=0), kc_gain)
    vc = jnp.concatenate([past_cmp[:, 1], jnp.mean(new_blocks(v_cmp), axis=1)], axis=0)
    q_pos = n_past + jnp.arange(S)
    nbt = nbp + nbn
    o_cmp, imp = _nsa_cmp(q, kc, vc, q_pos, jnp.arange(nbt) * NSA_BLOCK + NSA_BLOCK - 1, rel_bias)
    idx = _nsa_select(imp, q_pos, nbt)
    jp = jnp.minimum(idx, nbp - 1)
    rows = (jp % per_page)[..., None] * NSA_BLOCK + jnp.arange(NSA_BLOCK)
    past_rows = cache_sel[pages[jp // per_page][..., None], rows, :, jnp.arange(NSA_GROUPS)[None, :, None, None]]
    gi = jnp.arange(NSA_GROUPS)[None, :, None]
    jn = jnp.clip(idx - nbp, 0, nbn - 1)
    new_k = new_blocks(k_sel).transpose(2, 0, 1, 3)[gi, jn]
    new_v = new_blocks(v_sel).transpose(2, 0, 1, 3)[gi, jn]
    in_past = (idx < nbp)[..., None, None]
    k_rows = jnp.where(in_past, past_rows[..., 0, :], new_k)
    v_rows = jnp.where(in_past, past_rows[..., 1, :], new_v)
    o_sel = _nsa_sel(q, k_rows, v_rows, idx, q_pos, rel_bias)
    n_buf = win_buf.shape[0]
    kw = jnp.concatenate([win_buf[:, 0], k_win], axis=0)
    vw = jnp.concatenate([win_buf[:, 1], v_win], axis=0)
    o_win = _nsa_win(q, kw, vw, q_pos, n_past - n_buf + jnp.arange(n_buf + S), rel_bias)
    return _nsa_combine(gates, o_cmp, o_sel, o_win)


def _pool_mix(u_ext, pos_ext, n_new, w_pool, scale):
    N, L = u_ext.shape[:2]
    uf = u_ext.astype(jnp.float32)
    cs = jnp.concatenate([jnp.zeros_like(uf[:, :1]), jnp.cumsum(uf, axis=1)], axis=1)
    e = jnp.arange(L - n_new, L)
    parts = []
    for gi, w in enumerate(POOL_WINDOWS):
        c = slice(gi * POOL_GROUP, (gi + 1) * POOL_GROUP)
        win_sum = cs[:, e + 1, c] - cs[:, jnp.maximum(e + 1 - w, 0), c]
        cnt = jnp.minimum(w, pos_ext[e] + 1).astype(jnp.float32)[None, :, None]
        parts.append(win_sum / cnt - uf[:, e, c])
    d = jnp.stack(parts, axis=2).astype(u_ext.dtype)
    y = jnp.einsum('ntgc,gcd->ntgd', d, w_pool)
    return y.reshape(N, n_new, POOL_WIDTH) * scale


def _project(x, pos, p):
    z = _rmsnorm(x, p['attn_norm']) @ p['w_in']
    (u_pool, q_lat, kv_lat, k_pe, q_nsa, k_cmp, v_cmp, k_sel, v_sel, k_win, v_win, g_logit) = _split_in(z)
    N, L = x.shape[:2]
    grp = lambda a: a.reshape(N, L, NSA_GROUPS, NSA_DIM)
    q_mla = _mla_q(q_lat, p, pos)
    mla_rows = jnp.concatenate([_rmsnorm(kv_lat, p['mla_kv_norm']), k_pe], axis=-1)
    q_n = _rmsnorm(q_nsa.reshape(N, L, NSA_GROUPS, NSA_REP, NSA_DIM), p['nsa_q_gain'])
    k_sel = _rmsnorm(grp(k_sel), p['nsa_k_gain'][1])
    k_win = _rmsnorm(grp(k_win), p['nsa_k_gain'][2])
    gates = jax.nn.sigmoid(g_logit).reshape(N, L, 3, NSA_GROUPS, NSA_REP)
    return u_pool, q_mla, mla_rows, q_n, grp(k_cmp), grp(v_cmp), k_sel, grp(v_sel), k_win, grp(v_win), gates


def _finish(x, pool_o, mla_o, nsa_o, p):
    N, L = x.shape[:2]
    mix = jnp.concatenate([pool_o, mla_o.reshape(N, L, MLA_WIDTH), nsa_o.reshape(N, L, NSA_WIDTH)], axis=-1)
    x = x + mix @ p['w_out']
    h = _rmsnorm(x, p['ffn_norm'])
    return x + (jax.nn.silu(h @ p['w_gate']) * (h @ p['w_up'])) @ p['w_down']


def setup_inputs(seed: int = 0) -> dict:
    key = jax.random.key(seed)
    keys = jax.random.split(key, 26)
    nrm = lambda i, shape, scale=1.0: jax.random.normal(keys[i], shape, jnp.float32) * scale
    gain = lambda i, shape: 1.0 + 0.02 * jax.random.normal(keys[i], shape, jnp.float32)
    n_pages = PAST_LEN // PAGE_SIZE
    n_used = DEC_BATCH * n_pages
    n_pool = n_used + n_used // 4
    page_table = jax.random.permutation(keys[7], n_pool)[:n_used].reshape(DEC_BATCH, n_pages).astype(jnp.int32)
    kv_pages = (DEPTH, n_pool, PAGE_SIZE, 2, NSA_GROUPS, NSA_DIM)
    return {
        'x_prompt': nrm(0, (BATCH, SEQ, D_MODEL)),
        'x_sample': nrm(1, (DEC_BATCH, DEC_SEQ, D_MODEL)),
        'cache_mla': nrm(2, (DEPTH, n_pool, PAGE_SIZE, MLA_CACHE)),
        'cache_nsa_cmp': nrm(3, kv_pages),
        'cache_nsa_sel': nrm(4, kv_pages),
        'state_nsa_win': nrm(5, (DEPTH, DEC_BATCH, min(WINDOW, PAST_LEN), 2, NSA_GROUPS, NSA_DIM)),
        'state_pool': nrm(6, (DEPTH, DEC_BATCH, POOL_STATE, POOL_WIDTH)),
        'page_table': page_table,
        'rel_bias': nrm(8, (N_BUCKETS, NSA_HEADS), 0.5),
        'attn_norm': gain(9, (DEPTH, D_MODEL)),
        'w_in': nrm(10, (DEPTH, D_MODEL, IN_COLS), D_MODEL ** -0.5),
        'pool_w': nrm(11, (DEPTH, len(POOL_WINDOWS), POOL_GROUP, POOL_GROUP), POOL_GROUP ** -0.5),
        'pool_scale': gain(12, (DEPTH, POOL_WIDTH)),
        'mla_q_norm': gain(13, (DEPTH, Q_LORA)),
        'mla_kv_norm': gain(14, (DEPTH, KV_LORA)),
        'w_uq': nrm(15, (DEPTH, Q_LORA, MLA_HEADS * MLA_QK), Q_LORA ** -0.5),
        'w_ukv': nrm(16, (DEPTH, KV_LORA, MLA_HEADS * (MLA_NOPE + MLA_V)), KV_LORA ** -0.5),
        'mla_q_gain': gain(17, (DEPTH, MLA_QK)),
        'mla_k_gain': gain(18, (DEPTH, MLA_QK)),
        'nsa_q_gain': gain(19, (DEPTH, NSA_DIM)),
        'nsa_k_gain': gain(20, (DEPTH, 3, NSA_DIM)),
        'w_out': nrm(21, (DEPTH, MIX_WIDTH, D_MODEL), MIX_WIDTH ** -0.5),
        'ffn_norm': gain(22, (DEPTH, D_MODEL)),
        'w_gate': nrm(23, (DEPTH, D_MODEL, D_FF), D_MODEL ** -0.5),
        'w_up': nrm(24, (DEPTH, D_MODEL, D_FF), D_MODEL ** -0.5),
        'w_down': nrm(25, (DEPTH, D_FF, D_MODEL), D_FF ** -0.5),
    }


def reference(x_prompt, x_sample, cache_mla, cache_nsa_cmp, cache_nsa_sel, state_nsa_win, state_pool, page_table,
              rel_bias, attn_norm, w_in, pool_w, pool_scale, mla_q_norm, mla_kv_norm, w_uq, w_ukv,
              mla_q_gain, mla_k_gain, nsa_q_gain, nsa_k_gain, w_out, ffn_norm, w_gate, w_up, w_down):
    xp, xs = x_prompt, x_sample
    T, S = xp.shape[1], xs.shape[1]
    pos_p = jnp.arange(T)
    pos_s = PAST_LEN + jnp.arange(S)
    nmla_p, ncmp_p, nsel_p, nwin_p, npool_p = [], [], [], [], []
    nmla_s, ncmp_s, nsel_s, nwin_s, npool_s = [], [], [], [], []
    for l in range(DEPTH):
        p = {'attn_norm': attn_norm[l], 'w_in': w_in[l], 'mla_q_norm': mla_q_norm[l],
             'mla_kv_norm': mla_kv_norm[l], 'w_uq': w_uq[l], 'w_ukv': w_ukv[l],
             'mla_q_gain': mla_q_gain[l], 'mla_k_gain': mla_k_gain[l], 'nsa_q_gain': nsa_q_gain[l],
             'nsa_k_gain': nsa_k_gain[l], 'w_out': w_out[l], 'ffn_norm': ffn_norm[l],
             'w_gate': w_gate[l], 'w_up': w_up[l], 'w_down': w_down[l]}

        u, q_m, rows_m, q_n, kc, vc, ks, vs, kw, vw, gates = _project(xp, pos_p, p)
        pool_o = _pool_mix(u, pos_p, T, pool_w[l], pool_scale[l])
        k_m, v_m = _mla_kv(rows_m, p, pos_p)
        mla_o = jax.vmap(_mla_prompt_seq)(q_m, k_m, v_m)
        nsa_o = jax.vmap(_nsa_prompt_seq, in_axes=(0,) * 8 + (None, None))(
            q_n, kc, vc, ks, vs, kw, vw, gates, p['nsa_k_gain'][0], rel_bias)
        xp = _finish(xp, pool_o, mla_o, nsa_o, p)
        nmla_p.append(rows_m)
        ncmp_p.append(jnp.stack([kc, vc], axis=2))
        nsel_p.append(jnp.stack([ks, vs], axis=2))
        nwin_p.append(jnp.stack([kw, vw], axis=2)[:, T - min(WINDOW, T):])
        npool_p.append(u[:, T - POOL_STATE:])

        u, q_m, rows_m, q_n, kc, vc, ks, vs, kw, vw, gates = _project(xs, pos_s, p)
        u_ext = jnp.concatenate([state_pool[l], u], axis=1)
        n_ctx = state_pool.shape[2]
        pool_o = _pool_mix(u_ext, PAST_LEN - n_ctx + jnp.arange(n_ctx + S), S, pool_w[l], pool_scale[l])
        mla_o = lax.map(functools.partial(_mla_sample_seq, cache=cache_mla[l], p=p), (q_m, rows_m, page_table))
        nsa_o = lax.map(functools.partial(_nsa_sample_seq, cache_cmp=cache_nsa_cmp[l], cache_sel=cache_nsa_sel[l],
                                          kc_gain=p['nsa_k_gain'][0], rel_bias=rel_bias),
                        (q_n, kc, vc, ks, vs, kw, vw, gates, page_table, state_nsa_win[l]))
        xs = _finish(xs, pool_o, mla_o, nsa_o, p)
        n_buf = state_nsa_win.shape[2]
        nmla_s.append(rows_m)
        ncmp_s.append(jnp.stack([kc, vc], axis=2))
        nsel_s.append(jnp.stack([ks, vs], axis=2))
        nwin_s.append(jnp.concatenate([state_nsa_win[l], jnp.stack([kw, vw], axis=2)], axis=1)[:, S:S + n_buf])
        npool_s.append(u_ext[:, S:S + n_ctx])

    return (xp, xs,
            jnp.stack(nmla_p), jnp.stack(ncmp_p), jnp.stack(nsel_p), jnp.stack(nwin_p), jnp.stack(npool_p),
            jnp.stack(nmla_s), jnp.stack(ncmp_s), jnp.stack(nsel_s), jnp.stack(nwin_s), jnp.stack(npool_s))
```

```python
import functools
import math

import jax
import jax.numpy as jnp
import numpy as np
from jax import lax
from jax.experimental import pallas as pl
from jax.experimental.pallas import tpu as pltpu

D_MODEL = 2048
BATCH = 2
SEQ = 4096
DEPTH = 2
DEC_BATCH = 128
DEC_SEQ = 8
PAST_LEN = 16384
PAGE_SIZE = 128
POOL_WINDOWS = (2, 4, 8, 16)
POOL_WIDTH = D_MODEL // 4
POOL_GROUP = POOL_WIDTH // len(POOL_WINDOWS)
POOL_STATE = max(POOL_WINDOWS) - 1
MLA_HEADS = 6
MLA_NOPE = 64
MLA_ROPE = 32
MLA_QK = MLA_NOPE + MLA_ROPE
MLA_V = 128
MLA_WIDTH = MLA_HEADS * MLA_V
Q_LORA = 384
KV_LORA = 128
MLA_CACHE = KV_LORA + MLA_ROPE
ROPE_THETA = 10000.0
NSA_HEADS = 12
NSA_DIM = 64
NSA_GROUPS = 2
NSA_REP = NSA_HEADS // NSA_GROUPS
NSA_WIDTH = NSA_HEADS * NSA_DIM
NSA_KV = NSA_GROUPS * NSA_DIM
NSA_BLOCK = 64
NSA_TOPN = 16
WINDOW = 512
MIX_WIDTH = POOL_WIDTH + MLA_WIDTH + NSA_WIDTH
IN_SPLITS = (POOL_WIDTH, Q_LORA, KV_LORA, MLA_ROPE, NSA_WIDTH) + (NSA_KV,) * 6 + (3 * NSA_HEADS,)
IN_COLS = sum(IN_SPLITS)
N_BUCKETS = 32
MAX_DISTANCE = 1024
D_FF = -(-8 * D_MODEL // (3 * 256)) * 256
Q_BLOCK = 128
EPS = 1e-6
NEG = -1e30

VMEM_LIMIT_BYTES = 48 * 1024 * 1024


def _mm_kernel(x_ref, w_ref, o_ref):
    o_ref[...] = jnp.dot(x_ref[...], w_ref[...], preferred_element_type=jnp.float32).astype(o_ref.dtype)


def _pmm(x, w, tm=512, tn=512, out_dtype=jnp.float32):
    M, K = x.shape
    N = w.shape[1]
    tm = min(tm, M)
    assert M % tm == 0
    n_pad = -(-N // tn) * tn
    xb = x.astype(jnp.bfloat16)
    wb = w.astype(jnp.bfloat16)
    if n_pad != N:
        wb = jnp.pad(wb, ((0, 0), (0, n_pad - N)))
    out = pl.pallas_call(
        _mm_kernel,
        grid=(M // tm, n_pad // tn),
        in_specs=[pl.BlockSpec((tm, K), lambda i, j: (i, 0)),
                  pl.BlockSpec((K, tn), lambda i, j: (0, j))],
        out_specs=pl.BlockSpec((tm, tn), lambda i, j: (i, j)),
        out_shape=jax.ShapeDtypeStruct((M, n_pad), out_dtype),
        compiler_params=pltpu.CompilerParams(
            dimension_semantics=("parallel", "parallel"), vmem_limit_bytes=VMEM_LIMIT_BYTES),
        name="dense_mm",
    )(xb, wb)
    return out[:, :N] if n_pad != N else out


def _mm(x, w, **kw):
    lead = x.shape[:-1]
    return _pmm(x.reshape(-1, x.shape[-1]), w, **kw).reshape(lead + (w.shape[1],))


def _rmsnorm(x, g):
    xf = x.astype(jnp.float32)
    y = xf * lax.rsqrt(jnp.mean(xf * xf, axis=-1, keepdims=True) + EPS)
    return (y * g.astype(jnp.float32)).astype(x.dtype)


def _masked_softmax(s, mask):
    s = jnp.where(mask, s.astype(jnp.float32), NEG)
    p = jnp.exp(s - jnp.max(s, axis=-1, keepdims=True)) * mask
    return p / jnp.maximum(jnp.sum(p, axis=-1, keepdims=True), 1e-30)


def _rope(x, pos):
    half = x.shape[-1] // 2
    inv = ROPE_THETA ** (-jnp.arange(half, dtype=jnp.float32) / half)
    ang = pos.astype(jnp.float32)[:, None] * inv
    cos, sin = jnp.cos(ang)[:, None, :], jnp.sin(ang)[:, None, :]
    xf = x.astype(jnp.float32)
    x1, x2 = xf[..., :half], xf[..., half:]
    return jnp.concatenate([x1 * cos - x2 * sin, x1 * sin + x2 * cos], axis=-1).astype(x.dtype)


def _rel_bucket(dist):
    n = jnp.maximum(dist, 0)
    exact = N_BUCKETS // 2
    nf = jnp.maximum(n, exact).astype(jnp.float32)
    large = exact + (jnp.log(nf / exact) / math.log(MAX_DISTANCE / exact) * (N_BUCKETS - exact)).astype(jnp.int32)
    return jnp.where(n < exact, n, jnp.minimum(large, N_BUCKETS - 1))


def _bias_shared(rel_bias, dist):
    T, L = dist.shape
    b = rel_bias[_rel_bucket(dist)]
    return b.reshape(T, L, NSA_GROUPS, NSA_REP).transpose(0, 2, 3, 1)


def _bias_per_group(rel_bias, dist):
    bt = rel_bias.reshape(N_BUCKETS, NSA_GROUPS, NSA_REP)
    gi = jnp.arange(NSA_GROUPS)[None, :, None]
    return bt[_rel_bucket(dist), gi].transpose(0, 1, 3, 2)


def _split_in(z):
    cuts = [int(c) for c in np.cumsum(IN_SPLITS)[:-1]]
    return jnp.split(z, cuts, axis=-1)


def _mla_q(q_lat, p, pos):
    q = _mm(_rmsnorm(q_lat, p['mla_q_norm']), p['w_uq'])
    q = _rmsnorm(q.reshape(q.shape[:-1] + (MLA_HEADS, MLA_QK)), p['mla_q_gain'])
    return jnp.concatenate([q[..., :MLA_NOPE], _rope(q[..., MLA_NOPE:], pos)], axis=-1)


def _mla_kv(rows, p, pos):
    kv = (rows[..., :KV_LORA] @ p['w_ukv']).reshape(rows.shape[:-1] + (MLA_HEADS, MLA_NOPE + MLA_V))
    k_pe = jnp.broadcast_to(rows[..., None, KV_LORA:], kv.shape[:-1] + (MLA_ROPE,))
    k = _rmsnorm(jnp.concatenate([kv[..., :MLA_NOPE], k_pe], axis=-1), p['mla_k_gain'])
    k = jnp.concatenate([k[..., :MLA_NOPE], _rope(k[..., MLA_NOPE:], pos)], axis=-1)
    return k, kv[..., MLA_NOPE:]


def _mha(q, k, v, mask):
    s = jnp.einsum('thd,lhd->htl', q, k).astype(jnp.float32) * q.shape[-1] ** -0.5
    pr = _masked_softmax(s, mask[None])
    return jnp.einsum('htl,lhd->thd', pr.astype(v.dtype), v)


def _mla_prompt_seq(q, k, v):
    T = q.shape[0]
    k_pos = jnp.arange(T)

    def q_block(q0):
        qb = lax.dynamic_slice_in_dim(q, q0, Q_BLOCK)
        return _mha(qb, k, v, k_pos[None, :] <= (q0 + jnp.arange(Q_BLOCK))[:, None])

    return lax.map(q_block, jnp.arange(0, T, Q_BLOCK)).reshape(T, MLA_HEADS, MLA_V)


def _mla_sample_seq(args, cache, p):
    q, new_rows, pages = args
    S = q.shape[0]
    past = cache[pages].reshape(-1, MLA_CACHE)
    n_past = past.shape[0]
    k_pos = jnp.arange(n_past + S)
    k, v = _mla_kv(jnp.concatenate([past, new_rows], axis=0), p, k_pos)
    q_pos = n_past + jnp.arange(S)
    return _mha(q, k, v, k_pos[None, :] <= q_pos[:, None])


def _nsa_cmp(q, kc, vc, q_pos, blk_end, rel_bias):
    dist = q_pos[:, None] - blk_end[None, :]
    s = jnp.einsum('tgrd,ngd->tgrn', q, kc).astype(jnp.float32) * NSA_DIM ** -0.5 + _bias_shared(rel_bias, dist)
    pr = _masked_softmax(s, (dist >= 0)[:, None, None, :])
    o = jnp.einsum('tgrn,ngd->tgrd', pr.astype(vc.dtype), vc)
    return o, jnp.sum(pr, axis=2)


def _nsa_select(imp, q_pos, n_blocks):
    j = jnp.arange(n_blocks)
    cur = (q_pos // NSA_BLOCK)[:, None]
    forced = ((j == cur) | (j == 0))[:, None, :]
    allowed = (j <= cur)[:, None, :]
    score = jnp.where(allowed, jnp.where(forced, -NEG, imp.astype(jnp.float32)), NEG)
    _, idx = lax.top_k(score, min(NSA_TOPN, n_blocks))
    return idx


def _nsa_sel(q, k_rows, v_rows, idx, q_pos, rel_bias):
    T, G, K = idx.shape
    m = K * NSA_BLOCK
    k_pos = (idx[..., None] * NSA_BLOCK + jnp.arange(NSA_BLOCK)).reshape(T, G, m)
    dist = q_pos[:, None, None] - k_pos
    kr = k_rows.reshape(T, G, m, NSA_DIM)
    vr = v_rows.reshape(T, G, m, NSA_DIM)
    s = jnp.einsum('tgrd,tgmd->tgrm', q, kr).astype(jnp.float32) * NSA_DIM ** -0.5 + _bias_per_group(rel_bias, dist)
    pr = _masked_softmax(s, (dist >= 0)[:, :, None, :])
    return jnp.einsum('tgrm,tgmd->tgrd', pr.astype(vr.dtype), vr)


def _nsa_win(q, k, v, q_pos, k_pos, rel_bias):
    dist = q_pos[:, None] - k_pos[None, :]
    mask = (dist >= 0) & (dist < WINDOW) & (k_pos >= 0)[None, :]
    s = jnp.einsum('tgrd,lgd->tgrl', q, k).astype(jnp.float32) * NSA_DIM ** -0.5 + _bias_shared(rel_bias, dist)
    pr = _masked_softmax(s, mask[:, None, None, :])
    return jnp.einsum('tgrl,lgd->tgrd', pr.astype(v.dtype), v)


def _nsa_combine(gates, o_cmp, o_sel, o_win):
    g = gates[..., None]
    return g[:, 0] * o_cmp + g[:, 1] * o_sel + g[:, 2] * o_win


def _nsa_prompt_seq(q, k_cmp, v_cmp, k_sel, v_sel, k_win, v_win, gates, kc_gain, rel_bias):
    T = q.shape[0]
    nb = T // NSA_BLOCK
    blocks = lambda a: a.reshape(nb, NSA_BLOCK, NSA_GROUPS, NSA_DIM)
    pos = jnp.arange(T)
    kc = _rmsnorm(jnp.mean(blocks(k_cmp), axis=1), kc_gain)
    vc = jnp.mean(blocks(v_cmp), axis=1)
    o_cmp, imp = _nsa_cmp(q, kc, vc, pos, jnp.arange(nb) * NSA_BLOCK + NSA_BLOCK - 1, rel_bias)
    idx = _nsa_select(imp, pos, nb)
    ks_b = blocks(k_sel).transpose(2, 0, 1, 3)
    vs_b = blocks(v_sel).transpose(2, 0, 1, 3)
    gi = jnp.arange(NSA_GROUPS)[None, :, None]
    kw_pad = jnp.pad(k_win, ((WINDOW, 0), (0, 0), (0, 0)))
    vw_pad = jnp.pad(v_win, ((WINDOW, 0), (0, 0), (0, 0)))

    def q_block(q0):
        qb = lax.dynamic_slice_in_dim(q, q0, Q_BLOCK)
        pb = q0 + jnp.arange(Q_BLOCK)
        ib = lax.dynamic_slice_in_dim(idx, q0, Q_BLOCK)
        o_sel = _nsa_sel(qb, ks_b[gi, ib], vs_b[gi, ib], ib, pb, rel_bias)
        k_pos = q0 - WINDOW + jnp.arange(WINDOW + Q_BLOCK)
        o_win = _nsa_win(qb, lax.dynamic_slice_in_dim(kw_pad, q0, WINDOW + Q_BLOCK),
                         lax.dynamic_slice_in_dim(vw_pad, q0, WINDOW + Q_BLOCK), pb, k_pos, rel_bias)
        return o_sel, o_win

    o_sel, o_win = lax.map(q_block, jnp.arange(0, T, Q_BLOCK))
    shape = (T, NSA_GROUPS, NSA_REP, NSA_DIM)
    return _nsa_combine(gates, o_cmp, o_sel.reshape(shape), o_win.reshape(shape))


def _nsa_sample_seq(args, cache_cmp, cache_sel, kc_gain, rel_bias):
    q, k_cmp, v_cmp, k_sel, v_sel, k_win, v_win, gates, pages, win_buf = args
    S = q.shape[0]
    n_past = pages.shape[0] * PAGE_SIZE
    nbp = n_past // NSA_BLOCK
    nbn = -(-S // NSA_BLOCK)
    per_page = PAGE_SIZE // NSA_BLOCK
    pad = nbn * NSA_BLOCK - S
    new_blocks = lambda a: jnp.pad(a, ((0, pad), (0, 0), (0, 0))).reshape(nbn, NSA_BLOCK, NSA_GROUPS, NSA_DIM)
    past_cmp = jnp.mean(cache_cmp[pages].reshape(nbp, NSA_BLOCK, 2, NSA_GROUPS, NSA_DIM), axis=1)
    kc = _rmsnorm(jnp.concatenate([past_cmp[:, 0], jnp.mean(new_blocks(k_cmp), axis=1)], axis=0), kc_gain)
    vc = jnp.concatenate([past_cmp[:, 1], jnp.mean(new_blocks(v_cmp), axis=1)], axis=0)
    q_pos = n_past + jnp.arange(S)
    nbt = nbp + nbn
    o_cmp, imp = _nsa_cmp(q, kc, vc, q_pos, jnp.arange(nbt) * NSA_BLOCK + NSA_BLOCK - 1, rel_bias)
    idx = _nsa_select(imp, q_pos, nbt)
    jp = jnp.minimum(idx, nbp - 1)
    rows = (jp % per_page)[..., None] * NSA_BLOCK + jnp.arange(NSA_BLOCK)
    past_rows = cache_sel[pages[jp // per_page][..., None], rows, :, jnp.arange(NSA_GROUPS)[None, :, None, None]]
    gi = jnp.arange(NSA_GROUPS)[None, :, None]
    jn = jnp.clip(idx - nbp, 0, nbn - 1)
    new_k = new_blocks(k_sel).transpose(2, 0, 1, 3)[gi, jn]
    new_v = new_blocks(v_sel).transpose(2, 0, 1, 3)[gi, jn]
    in_past = (idx < nbp)[..., None, None]
    k_rows = jnp.where(in_past, past_rows[..., 0, :], new_k)
    v_rows = jnp.where(in_past, past_rows[..., 1, :], new_v)
    o_sel = _nsa_sel(q, k_rows, v_rows, idx, q_pos, rel_bias)
    n_buf = win_buf.shape[0]
    kw = jnp.concatenate([win_buf[:, 0], k_win], axis=0)
    vw = jnp.concatenate([win_buf[:, 1], v_win], axis=0)
    o_win = _nsa_win(q, kw, vw, q_pos, n_past - n_buf + jnp.arange(n_buf + S), rel_bias)
    return _nsa_combine(gates, o_cmp, o_sel, o_win)


def _pool_mix(u_ext, pos_ext, n_new, w_pool, scale):
    N, L = u_ext.shape[:2]
    uf = u_ext.astype(jnp.float32)
    cs = jnp.concatenate([jnp.zeros_like(uf[:, :1]), jnp.cumsum(uf, axis=1)], axis=1)
    e = jnp.arange(L - n_new, L)
    parts = []
    for gi, w in enumerate(POOL_WINDOWS):
        c = slice(gi * POOL_GROUP, (gi + 1) * POOL_GROUP)
        win_sum = cs[:, e + 1, c] - cs[:, jnp.maximum(e + 1 - w, 0), c]
        cnt = jnp.minimum(w, pos_ext[e] + 1).astype(jnp.float32)[None, :, None]
        parts.append(win_sum / cnt - uf[:, e, c])
    d = jnp.stack(parts, axis=2).astype(u_ext.dtype)
    y = jnp.einsum('ntgc,gcd->ntgd', d, w_pool)
    return y.reshape(N, n_new, POOL_WIDTH) * scale


def _project(x, pos, p):
    z = _mm(_rmsnorm(x, p['attn_norm']), p['w_in'], tn=384)
    (u_pool, q_lat, kv_lat, k_pe, q_nsa, k_cmp, v_cmp, k_sel, v_sel, k_win, v_win, g_logit) = _split_in(z)
    N, L = x.shape[:2]
    grp = lambda a: a.reshape(N, L, NSA_GROUPS, NSA_DIM)
    q_mla = _mla_q(q_lat, p, pos)
    mla_rows = jnp.concatenate([_rmsnorm(kv_lat, p['mla_kv_norm']), k_pe], axis=-1)
    q_n = _rmsnorm(q_nsa.reshape(N, L, NSA_GROUPS, NSA_REP, NSA_DIM), p['nsa_q_gain'])
    k_sel = _rmsnorm(grp(k_sel), p['nsa_k_gain'][1])
    k_win = _rmsnorm(grp(k_win), p['nsa_k_gain'][2])
    gates = jax.nn.sigmoid(g_logit).reshape(N, L, 3, NSA_GROUPS, NSA_REP)
    return u_pool, q_mla, mla_rows, q_n, grp(k_cmp), grp(v_cmp), k_sel, grp(v_sel), k_win, grp(v_win), gates


def _finish(x, pool_o, mla_o, nsa_o, p):
    N, L = x.shape[:2]
    mix = jnp.concatenate([pool_o, mla_o.reshape(N, L, MLA_WIDTH), nsa_o.reshape(N, L, NSA_WIDTH)], axis=-1)
    x = x + _mm(mix, p['w_out'])
    h = _rmsnorm(x, p['ffn_norm'])
    a = jax.nn.silu(_mm(h, p['w_gate'])) * _mm(h, p['w_up'])
    return x + _mm(a, p['w_down'])


def kernel(x_prompt, x_sample, cache_mla, cache_nsa_cmp, cache_nsa_sel, state_nsa_win, state_pool, page_table,
           rel_bias, attn_norm, w_in, pool_w, pool_scale, mla_q_norm, mla_kv_norm, w_uq, w_ukv,
           mla_q_gain, mla_k_gain, nsa_q_gain, nsa_k_gain, w_out, ffn_norm, w_gate, w_up, w_down):
    xp, xs = x_prompt, x_sample
    T, S = xp.shape[1], xs.shape[1]
    pos_p = jnp.arange(T)
    pos_s = PAST_LEN + jnp.arange(S)
    nmla_p, ncmp_p, nsel_p, nwin_p, npool_p = [], [], [], [], []
    nmla_s, ncmp_s, nsel_s, nwin_s, npool_s = [], [], [], [], []
    for l in range(DEPTH):
        p = {'attn_norm': attn_norm[l], 'w_in': w_in[l], 'mla_q_norm': mla_q_norm[l],
             'mla_kv_norm': mla_kv_norm[l], 'w_uq': w_uq[l], 'w_ukv': w_ukv[l],
             'mla_q_gain': mla_q_gain[l], 'mla_k_gain': mla_k_gain[l], 'nsa_q_gain': nsa_q_gain[l],
             'nsa_k_gain': nsa_k_gain[l], 'w_out': w_out[l], 'ffn_norm': ffn_norm[l],
             'w_gate': w_gate[l], 'w_up': w_up[l], 'w_down': w_down[l]}

        u, q_m, rows_m, q_n, kc, vc, ks, vs, kw, vw, gates = _project(xp, pos_p, p)
        pool_o = _pool_mix(u, pos_p, T, pool_w[l], pool_scale[l])
        k_m, v_m = _mla_kv(rows_m, p, pos_p)
        mla_o = jax.vmap(_mla_prompt_seq)(q_m, k_m, v_m)
        nsa_o = jax.vmap(_nsa_prompt_seq, in_axes=(0,) * 8 + (None, None))(
            q_n, kc, vc, ks, vs, kw, vw, gates, p['nsa_k_gain'][0], rel_bias)
        xp = _finish(xp, pool_o, mla_o, nsa_o, p)
        nmla_p.append(rows_m)
        ncmp_p.append(jnp.stack([kc, vc], axis=2))
        nsel_p.append(jnp.stack([ks, vs], axis=2))
        nwin_p.append(jnp.stack([kw, vw], axis=2)[:, T - min(WINDOW, T):])
        npool_p.append(u[:, T - POOL_STATE:])

        u, q_m, rows_m, q_n, kc, vc, ks, vs, kw, vw, gates = _project(xs, pos_s, p)
        u_ext = jnp.concatenate([state_pool[l], u], axis=1)
        n_ctx = state_pool.shape[2]
        pool_o = _pool_mix(u_ext, PAST_LEN - n_ctx + jnp.arange(n_ctx + S), S, pool_w[l], pool_scale[l])
        mla_o = lax.map(functools.partial(_mla_sample_seq, cache=cache_mla[l], p=p), (q_m, rows_m, page_table))
        nsa_o = lax.map(functools.partial(_nsa_sample_seq, cache_cmp=cache_nsa_cmp[l], cache_sel=cache_nsa_sel[l],
                                          kc_gain=p['nsa_k_gain'][0], rel_bias=rel_bias),
                        (q_n, kc, vc, ks, vs, kw, vw, gates, page_table, state_nsa_win[l]))
        xs = _finish(xs, pool_o, mla_o, nsa_o, p)
        n_buf = state_nsa_win.shape[2]
        nmla_s.append(rows_m)
        ncmp_s.append(jnp.stack([kc, vc], axis=2))
        nsel_s.append(jnp.stack([ks, vs], axis=2))
        nwin_s.append(jnp.concatenate([state_nsa_win[l], jnp.stack([kw, vw], axis=2)], axis=1)[:, S:S + n_buf])
        npool_s.append(u_ext[:, S:S + n_ctx])

    return (xp, xs,
            jnp.stack(nmla_p), jnp.stack(ncmp_p), jnp.stack(nsel_p), jnp.stack(nwin_p), jnp.stack(npool_p),
            jnp.stack(nmla_s), jnp.stack(ncmp_s), jnp.stack(nsel_s), jnp.stack(nwin_s), jnp.stack(npool_s))
```

```python
import functools
import math

import jax
import jax.numpy as jnp
import numpy as np
from jax import lax
from jax.experimental import pallas as pl
from jax.experimental.pallas import tpu as pltpu

D_MODEL = 2048
BATCH = 2
SEQ = 4096
DEPTH = 2
DEC_BATCH = 128
DEC_SEQ = 8
PAST_LEN = 16384
PAGE_SIZE = 128
POOL_WINDOWS = (2, 4, 8, 16)
POOL_WIDTH = D_MODEL // 4
POOL_GROUP = POOL_WIDTH // len(POOL_WINDOWS)
POOL_STATE = max(POOL_WINDOWS) - 1
MLA_HEADS = 6
MLA_NOPE = 64
MLA_ROPE = 32
MLA_QK = MLA_NOPE + MLA_ROPE
MLA_V = 128
MLA_WIDTH = MLA_HEADS * MLA_V
Q_LORA = 384
KV_LORA = 128
MLA_CACHE = KV_LORA + MLA_ROPE
ROPE_THETA = 10000.0
NSA_HEADS = 12
NSA_DIM = 64
NSA_GROUPS = 2
NSA_REP = NSA_HEADS // NSA_GROUPS
NSA_WIDTH = NSA_HEADS * NSA_DIM
NSA_KV = NSA_GROUPS * NSA_DIM
NSA_BLOCK = 64
NSA_TOPN = 16
WINDOW = 512
MIX_WIDTH = POOL_WIDTH + MLA_WIDTH + NSA_WIDTH
IN_SPLITS = (POOL_WIDTH, Q_LORA, KV_LORA, MLA_ROPE, NSA_WIDTH) + (NSA_KV,) * 6 + (3 * NSA_HEADS,)
IN_COLS = sum(IN_SPLITS)
N_BUCKETS = 32
MAX_DISTANCE = 1024
D_FF = -(-8 * D_MODEL // (3 * 256)) * 256
Q_BLOCK = 128
EPS = 1e-6
NEG = -1e30

LANES = 128
KV_LANES = 2 * NSA_KV
VMEM_LIMIT_BYTES = 48 * 1024 * 1024
QT = 128
N_DELTA = 9
MLA_CHUNK_PAGES = 16
NSA_CHUNK_PAGES = 16
ROWS_Q = NSA_REP * DEC_SEQ
MLA_ROWS_Q = MLA_HEADS * DEC_SEQ


def _mm_kernel(x_ref, w_ref, o_ref):
    o_ref[...] = jnp.dot(x_ref[...], w_ref[...], preferred_element_type=jnp.float32).astype(o_ref.dtype)


def _pmm(x, w, tm=512, tn=512, out_dtype=jnp.float32):
    M, K = x.shape
    N = w.shape[1]
    tm = min(tm, M)
    assert M % tm == 0
    n_pad = -(-N // tn) * tn
    xb = x.astype(jnp.bfloat16)
    wb = w.astype(jnp.bfloat16)
    if n_pad != N:
        wb = jnp.pad(wb, ((0, 0), (0, n_pad - N)))
    out = pl.pallas_call(
        _mm_kernel,
        grid=(M // tm, n_pad // tn),
        in_specs=[pl.BlockSpec((tm, K), lambda i, j: (i, 0)),
                  pl.BlockSpec((K, tn), lambda i, j: (0, j))],
        out_specs=pl.BlockSpec((tm, tn), lambda i, j: (i, j)),
        out_shape=jax.ShapeDtypeStruct((M, n_pad), out_dtype),
        compiler_params=pltpu.CompilerParams(
            dimension_semantics=("parallel", "parallel"), vmem_limit_bytes=VMEM_LIMIT_BYTES),
        name="dense_mm",
    )(xb, wb)
    return out[:, :N] if n_pad != N else out


def _mm(x, w, **kw):
    lead = x.shape[:-1]
    return _pmm(x.reshape(-1, x.shape[-1]), w, **kw).reshape(lead + (w.shape[1],))


def _rmsnorm(x, g):
    xf = x.astype(jnp.float32)
    y = xf * lax.rsqrt(jnp.mean(xf * xf, axis=-1, keepdims=True) + EPS)
    return (y * g.astype(jnp.float32)).astype(x.dtype)


def _masked_softmax(s, mask):
    s = jnp.where(mask, s.astype(jnp.float32), NEG)
    p = jnp.exp(s - jnp.max(s, axis=-1, keepdims=True)) * mask
    return p / jnp.maximum(jnp.sum(p, axis=-1, keepdims=True), 1e-30)


def _rope_tables(n_pos):
    half = MLA_ROPE // 2
    inv = ROPE_THETA ** (-jnp.arange(half, dtype=jnp.float32) / half)
    ang = jnp.arange(n_pos).astype(jnp.float32)[:, None] * inv
    return jnp.cos(ang), jnp.sin(ang)


def _rope(x, pos):
    half = x.shape[-1] // 2
    inv = ROPE_THETA ** (-jnp.arange(half, dtype=jnp.float32) / half)
    ang = pos.astype(jnp.float32)[:, None] * inv
    cos, sin = jnp.cos(ang)[:, None, :], jnp.sin(ang)[:, None, :]
    xf = x.astype(jnp.float32)
    x1, x2 = xf[..., :half], xf[..., half:]
    return jnp.concatenate([x1 * cos - x2 * sin, x1 * sin + x2 * cos], axis=-1).astype(x.dtype)


def _rel_bucket(dist):
    n = jnp.maximum(dist, 0)
    exact = N_BUCKETS // 2
    nf = jnp.maximum(n, exact).astype(jnp.float32)
    large = exact + (jnp.log(nf / exact) / math.log(MAX_DISTANCE / exact) * (N_BUCKETS - exact)).astype(jnp.int32)
    return jnp.where(n < exact, n, jnp.minimum(large, N_BUCKETS - 1))


def _split_in(z):
    cuts = [int(c) for c in np.cumsum(IN_SPLITS)[:-1]]
    return jnp.split(z, cuts, axis=-1)


def _mla_q(q_lat, p, pos):
    q = _mm(_rmsnorm(q_lat, p['mla_q_norm']), p['w_uq'])
    q = _rmsnorm(q.reshape(q.shape[:-1] + (MLA_HEADS, MLA_QK)), p['mla_q_gain'])
    return jnp.concatenate([q[..., :MLA_NOPE], _rope(q[..., MLA_NOPE:], pos)], axis=-1)


def _mla_kv(rows, p, pos):
    kv = (rows[..., :KV_LORA] @ p['w_ukv']).reshape(rows.shape[:-1] + (MLA_HEADS, MLA_NOPE + MLA_V))
    k_pe = jnp.broadcast_to(rows[..., None, KV_LORA:], kv.shape[:-1] + (MLA_ROPE,))
    k = _rmsnorm(jnp.concatenate([kv[..., :MLA_NOPE], k_pe], axis=-1), p['mla_k_gain'])
    k = jnp.concatenate([k[..., :MLA_NOPE], _rope(k[..., MLA_NOPE:], pos)], axis=-1)
    return k, kv[..., MLA_NOPE:]


def _mha(q, k, v, mask):
    s = jnp.einsum('thd,lhd->htl', q, k).astype(jnp.float32) * q.shape[-1] ** -0.5
    pr = _masked_softmax(s, mask[None])
    return jnp.einsum('htl,lhd->thd', pr.astype(v.dtype), v)


def _mla_prompt_seq(q, k, v):
    T = q.shape[0]
    k_pos = jnp.arange(T)

    def q_block(q0):
        qb = lax.dynamic_slice_in_dim(q, q0, Q_BLOCK)
        return _mha(qb, k, v, k_pos[None, :] <= (q0 + jnp.arange(Q_BLOCK))[:, None])

    return lax.map(q_block, jnp.arange(0, T, Q_BLOCK)).reshape(T, MLA_HEADS, MLA_V)


def _nsa_combine(gates, o_cmp, o_sel, o_win):
    g = gates[..., None]
    return g[:, :, 0] * o_cmp + g[:, :, 1] * o_sel + g[:, :, 2] * o_win


def _blocksum_kernel(k_ref, v_ref, gain_ref, kc_ref, vc_ref):
    nb = kc_ref.shape[1]
    k = k_ref[0].reshape(nb, NSA_BLOCK, LANES)
    v = v_ref[0].reshape(nb, NSA_BLOCK, LANES)
    km = jnp.sum(k, axis=1) * (1.0 / NSA_BLOCK)
    vm = jnp.sum(v, axis=1) * (1.0 / NSA_BLOCK)
    lo = lax.broadcasted_iota(jnp.int32, km.shape, 1) < NSA_DIM
    sq = km * km
    ss_lo = jnp.sum(jnp.where(lo, sq, 0.0), axis=-1, keepdims=True)
    ss_hi = jnp.sum(jnp.where(lo, 0.0, sq), axis=-1, keepdims=True)
    ss = jnp.where(lo, ss_lo, ss_hi)
    kc_ref[0] = km * lax.rsqrt(ss * (1.0 / NSA_DIM) + EPS) * gain_ref[...]
    vc_ref[0] = vm


def _block_summaries(k_cmp, v_cmp, kc_gain):
    B, T, _ = k_cmp.shape
    nb = T // NSA_BLOCK
    gain = jnp.tile(kc_gain.astype(jnp.float32), NSA_GROUPS).reshape(1, LANES)
    spec = pl.BlockSpec((1, T, LANES), lambda b: (b, 0, 0))
    ospec = pl.BlockSpec((1, nb, LANES), lambda b: (b, 0, 0))
    return pl.pallas_call(
        _blocksum_kernel,
        grid=(B,),
        in_specs=[spec, spec, pl.BlockSpec((1, LANES), lambda b: (0, 0))],
        out_specs=[ospec, ospec],
        out_shape=[jax.ShapeDtypeStruct((B, nb, LANES), jnp.float32)] * 2,
        compiler_params=pltpu.CompilerParams(dimension_semantics=("parallel",), vmem_limit_bytes=VMEM_LIMIT_BYTES),
        name="nsa_block_summaries",
    )(k_cmp, v_cmp, gain)


def _nsa_prompt_kernel(q_ref, kc_ref, vc_ref, bcmp_ref, ks_ref, vs_ref, kw_ref, vw_ref, btile_ref,
                       ocmp_ref, osel_ref, owin_ref, mexp_ref, m_ref, l_ref, acc_ref):
    qi = pl.program_id(1)
    T = ks_ref.shape[1]
    nb = kc_ref.shape[1]
    n_kt = T // QT
    scale = NSA_DIM ** -0.5
    rows = NSA_REP * QT

    tq = lax.broadcasted_iota(jnp.int32, (QT, nb), 0) + qi * QT
    nidx = lax.broadcasted_iota(jnp.int32, (QT, nb), 1)
    ii = lax.broadcasted_iota(jnp.int32, (QT, QT), 0)
    jj = lax.broadcasted_iota(jnp.int32, (QT, QT), 1)
    expand = (lax.broadcasted_iota(jnp.int32, (nb, T), 1) // NSA_BLOCK
              == lax.broadcasted_iota(jnp.int32, (nb, T), 0)).astype(jnp.bfloat16)
    lane_lo = lax.broadcasted_iota(jnp.int32, (QT, LANES), 1) < NSA_DIM

    def flash(g, qb, k_ref, v_ref, lo, windowed):
        m_ref[...] = jnp.full(m_ref.shape, NEG, jnp.float32)
        l_ref[...] = jnp.zeros(l_ref.shape, jnp.float32)
        acc_ref[...] = jnp.zeros(acc_ref.shape, jnp.float32)

        def body(ki, carry):
            k0 = pl.multiple_of(ki * QT, QT)
            k = k_ref[0, pl.ds(k0, QT), :]
            v = v_ref[0, pl.ds(k0, QT), :]
            s = lax.dot_general(qb, k, (((1,), (1,)), ((), ())), preferred_element_type=jnp.float32)
            delta = qi - ki
            dist = delta * QT + ii - jj
            if windowed:
                mask = (dist >= 0) & (dist < WINDOW)
            else:
                mask = (dist >= 0) & (mexp_ref[g, ki] > 0.5)
            maskf = mask.astype(jnp.float32)
            dcl = jnp.minimum(delta, N_DELTA - 1)
            ps, alphas = [], []
            for r in range(NSA_REP):
                sl = slice(r * QT, (r + 1) * QT)
                sr = s[sl] * scale + btile_ref[g, r, dcl]
                sr = jnp.where(mask, sr, NEG)
                m_old = m_ref[sl]
                m_new = jnp.maximum(m_old, jnp.max(sr, axis=-1, keepdims=True))
                p = jnp.exp(sr - m_new) * maskf
                alpha = jnp.exp(m_old - m_new)
                l_ref[sl] = alpha * l_ref[sl] + jnp.sum(p, axis=-1, keepdims=True)
                m_ref[sl] = m_new
                ps.append(p.astype(jnp.bfloat16))
                alphas.append(alpha)
            pv = jnp.dot(jnp.concatenate(ps, axis=0), v, preferred_element_type=jnp.float32)
            acc_ref[...] = jnp.concatenate(alphas, axis=0) * acc_ref[...] + pv
            return carry

        lax.fori_loop(lo, qi + 1, body, 0)
        return acc_ref[...] / jnp.maximum(l_ref[...], 1e-30)

    outs = []
    for g in range(NSA_GROUPS):
        qf = q_ref[0, g].reshape(rows, LANES)
        qb = qf.astype(jnp.bfloat16)
        s = lax.dot_general(qf, kc_ref[0], (((1,), (1,)), ((), ())),
                            precision=lax.Precision.HIGHEST, preferred_element_type=jnp.float32)
        s = s * scale + bcmp_ref[g].reshape(rows, nb)
        cmask = jnp.concatenate([tq >= nidx * NSA_BLOCK + NSA_BLOCK - 1] * NSA_REP, axis=0)
        cmaskf = cmask.astype(jnp.float32)
        s = jnp.where(cmask, s, NEG)
        p = jnp.exp(s - jnp.max(s, axis=-1, keepdims=True)) * cmaskf
        pr = p / jnp.maximum(jnp.sum(p, axis=-1, keepdims=True), 1e-30)
        o_cmp = jnp.dot(pr.astype(jnp.bfloat16), vc_ref[0].astype(jnp.bfloat16), preferred_element_type=jnp.float32)
        imp = pr[0:QT]
        for r in range(1, NSA_REP):
            imp = imp + pr[r * QT:(r + 1) * QT]
        cur = tq // NSA_BLOCK
        forced = (nidx == cur) | (nidx == 0)
        score = jnp.where(nidx <= cur, jnp.where(forced, -NEG, imp), NEG)
        cnt = jnp.zeros((QT, nb), jnp.float32)
        for j in range(nb):
            col = score[:, j:j + 1]
            beats = (col > score) | ((col == score) & (nidx > j))
            cnt = cnt + beats.astype(jnp.float32)
        msel = (cnt < float(min(NSA_TOPN, nb))).astype(jnp.bfloat16)
        mexp = jnp.dot(msel, expand, preferred_element_type=jnp.float32)
        for kk in range(n_kt):
            mexp_ref[g, kk] = mexp[:, kk * QT:(kk + 1) * QT]
        o_sel = flash(g, qb, ks_ref, vs_ref, 0, False)
        o_win = flash(g, qb, kw_ref, vw_ref, jnp.maximum(qi - WINDOW // QT, 0), True)
        outs.append((o_cmp, o_sel, o_win))

    for r in range(NSA_REP):
        sl = slice(r * QT, (r + 1) * QT)
        ocmp_ref[0, r] = jnp.where(lane_lo, outs[0][0][sl], outs[1][0][sl])
        osel_ref[0, r] = jnp.where(lane_lo, outs[0][1][sl], outs[1][1][sl])
        owin_ref[0, r] = jnp.where(lane_lo, outs[0][2][sl], outs[1][2][sl])


def _nsa_prompt_tables(rel_bias, T):
    nb = T // NSA_BLOCK
    bt = rel_bias.astype(jnp.float32).reshape(N_BUCKETS, NSA_GROUPS, NSA_REP)
    d = (jnp.arange(N_DELTA)[:, None, None] * QT + jnp.arange(QT)[None, :, None] - jnp.arange(QT)[None, None, :])
    tiles = bt[_rel_bucket(d)].transpose(3, 4, 0, 1, 2)
    dc = jnp.arange(T)[:, None] - (jnp.arange(nb)[None, :] * NSA_BLOCK + NSA_BLOCK - 1)
    bcmp = bt[_rel_bucket(dc)].transpose(2, 3, 0, 1)
    return tiles, bcmp


def _group_padded_queries(qt):
    z = jnp.zeros_like(qt[:, 0])
    return jnp.stack([jnp.concatenate([qt[:, 0], z], axis=-1), jnp.concatenate([z, qt[:, 1]], axis=-1)], axis=1)


def _nsa_prompt(q_n, k_cmp, v_cmp, k_sel, v_sel, k_win, v_win, kc_gain, tables):
    B, T = q_n.shape[:2]
    assert T % QT == 0 and QT == 2 * NSA_BLOCK
    nb = T // NSA_BLOCK
    tiles, bcmp = tables
    kc, vc = _block_summaries(k_cmp, v_cmp, kc_gain)
    qt = q_n.transpose(0, 2, 3, 1, 4).reshape(B, NSA_GROUPS, NSA_REP * T, NSA_DIM)
    q_pad = _group_padded_queries(qt).reshape(B, NSA_GROUPS, NSA_REP, T, LANES)
    bf = lambda a: a.astype(jnp.bfloat16)
    kv_spec = pl.BlockSpec((1, T, LANES), lambda b, i: (b, 0, 0))
    sm_spec = pl.BlockSpec((1, nb, LANES), lambda b, i: (b, 0, 0))
    o_spec = pl.BlockSpec((1, NSA_REP, QT, LANES), lambda b, i: (b, 0, i, 0))
    o_shape = jax.ShapeDtypeStruct((B, NSA_REP, T, LANES), jnp.float32)
    outs = pl.pallas_call(
        _nsa_prompt_kernel,
        grid=(B, T // QT),
        in_specs=[pl.BlockSpec((1, NSA_GROUPS, NSA_REP, QT, LANES), lambda b, i: (b, 0, 0, i, 0)),
                  sm_spec, sm_spec,
                  pl.BlockSpec((NSA_GROUPS, NSA_REP, QT, nb), lambda b, i: (0, 0, i, 0)),
                  kv_spec, kv_spec, kv_spec, kv_spec,
                  pl.BlockSpec((NSA_GROUPS, NSA_REP, N_DELTA, QT, QT), lambda b, i: (0, 0, 0, 0, 0))],
        out_specs=[o_spec, o_spec, o_spec],
        out_shape=[o_shape, o_shape, o_shape],
        scratch_shapes=[pltpu.VMEM((NSA_GROUPS, T // QT, QT, QT), jnp.float32),
                        pltpu.VMEM((NSA_REP * QT, 1), jnp.float32),
                        pltpu.VMEM((NSA_REP * QT, 1), jnp.float32),
                        pltpu.VMEM((NSA_REP * QT, LANES), jnp.float32)],
        compiler_params=pltpu.CompilerParams(
            dimension_semantics=("parallel", "arbitrary"), vmem_limit_bytes=VMEM_LIMIT_BYTES),
        name="nsa_prompt_attention",
    )(q_pad, kc, vc, bcmp, bf(k_sel), bf(v_sel), bf(k_win), bf(v_win), tiles)
    back = lambda o: o.reshape(B, NSA_REP, T, NSA_GROUPS, NSA_DIM).transpose(0, 2, 3, 1, 4)
    return tuple(back(o) for o in outs)


def _mla_sample_kernel(layer, n_pages, pt_ref, cache_ref, new_ref, qn_ref, qpe_ref, gn_ref, wuk_ref, wuv_ref,
                       eye_ref, tbl_ref, o_ref, buf_ref, sem_ref, m_ref, l_ref, acc_ref):
    b = pl.program_id(0)
    n_seq = pl.num_programs(0)
    cp = MLA_CHUNK_PAGES
    lc = cp * PAGE_SIZE
    n_chunks = n_pages // cp
    scale = MLA_QK ** -0.5

    def page_copy(seq, chunk, p, slot):
        page = pt_ref[seq, chunk * cp + p]
        return pltpu.make_async_copy(cache_ref.at[layer, page], buf_ref.at[slot, pl.ds(p * PAGE_SIZE, PAGE_SIZE)],
                                     sem_ref.at[slot])

    def start_chunk(seq, chunk, slot):
        for p in range(cp):
            page_copy(seq, chunk, p, slot).start()

    def wait_chunk(seq, chunk, slot):
        for p in range(cp):
            page_copy(seq, chunk, p, slot).wait()

    @pl.when(b == 0)
    def _():
        start_chunk(0, 0, 0)

    m_ref[...] = jnp.full(m_ref.shape, NEG, jnp.float32)
    l_ref[...] = jnp.zeros(l_ref.shape, jnp.float32)
    acc_ref[...] = jnp.zeros(acc_ref.shape, jnp.float32)

    qn = (qn_ref[0] * gn_ref[...]).astype(jnp.bfloat16)
    a_abs = jnp.dot(qn, wuk_ref[...], preferred_element_type=jnp.float32)
    lhs1 = jnp.concatenate([wuk_ref[...], a_abs.astype(jnp.bfloat16)], axis=0)
    qpe = qpe_ref[0].astype(jnp.bfloat16)
    n_kn = MLA_HEADS * MLA_NOPE

    def attend(rows, col0, mask):
        L = rows.shape[0]
        lat = rows[:, :KV_LORA].astype(jnp.bfloat16)
        kpe = rows[:, KV_LORA:].astype(jnp.bfloat16)
        x = lax.dot_general(lhs1, lat, (((1,), (1,)), ((), ())), preferred_element_type=jnp.float32)
        gx = lax.dot_general(eye_ref[...], kpe, (((1,), (1,)), ((), ())), preferred_element_type=jnp.float32)
        ss_pe = jnp.sum(gx[:MLA_ROPE] * gx[:MLA_ROPE], axis=0, keepdims=True)
        f = (gx * tbl_ref[:, pl.ds(col0, L)]).astype(jnp.bfloat16)
        s_pe = jnp.dot(qpe, f, preferred_element_type=jnp.float32)
        ps, alphas = [], []
        for h in range(MLA_HEADS):
            kn = x[h * MLA_NOPE:(h + 1) * MLA_NOPE]
            ss = jnp.sum(kn * kn, axis=0, keepdims=True) + ss_pe
            r = lax.rsqrt(ss * (1.0 / MLA_QK) + EPS) * scale
            sl = slice(h * DEC_SEQ, (h + 1) * DEC_SEQ)
            s = (x[n_kn + h * DEC_SEQ:n_kn + (h + 1) * DEC_SEQ] + s_pe[sl]) * r
            if mask is not None:
                s = jnp.where(mask, s, NEG)
            m_old = m_ref[sl]
            m_new = jnp.maximum(m_old, jnp.max(s, axis=-1, keepdims=True))
            p = jnp.exp(s - m_new)
            if mask is not None:
                p = p * mask.astype(jnp.float32)
            alpha = jnp.exp(m_old - m_new)
            l_ref[sl] = alpha * l_ref[sl] + jnp.sum(p, axis=-1, keepdims=True)
            m_ref[sl] = m_new
            ps.append(p.astype(jnp.bfloat16))
            alphas.append(alpha)
        pv = jnp.dot(jnp.concatenate(ps, axis=0), lat, preferred_element_type=jnp.float32)
        acc_ref[...] = jnp.concatenate(alphas, axis=0) * acc_ref[...] + pv

    def chunk_body(c, carry):
        slot = c % 2

        @pl.when(c + 1 < n_chunks)
        def _():
            start_chunk(b, c + 1, 1 - slot)

        @pl.when((c + 1 == n_chunks) & (b + 1 < n_seq))
        def _():
            start_chunk(b + 1, 0, 1 - slot)

        wait_chunk(b, c, slot)
        attend(buf_ref[slot], pl.multiple_of(c * lc, lc), None)
        return carry

    lax.fori_loop(0, n_chunks, chunk_body, 0)

    tt = lax.broadcasted_iota(jnp.int32, (DEC_SEQ, PAGE_SIZE), 0)
    jj = lax.broadcasted_iota(jnp.int32, (DEC_SEQ, PAGE_SIZE), 1)
    attend(new_ref[0], n_pages * PAGE_SIZE, jj <= tt)

    o_lat = (acc_ref[...] / jnp.maximum(l_ref[...], 1e-30)).astype(jnp.bfloat16)
    for h in range(MLA_HEADS):
        o_ref[0, h * DEC_SEQ:(h + 1) * DEC_SEQ, :] = jnp.dot(o_lat[h * DEC_SEQ:(h + 1) * DEC_SEQ], wuv_ref[h],
                                                             preferred_element_type=jnp.float32)


def _mla_sample(q_m, rows_new, page_table, cache_mla, layer, w_ukv, k_gain, cos, sin):
    DB, S = q_m.shape[:2]
    n_pages = page_table.shape[1]
    assert S == DEC_SEQ and n_pages % MLA_CHUNK_PAGES == 0 and (n_pages // MLA_CHUNK_PAGES) % 2 == 0
    n_pos = n_pages * PAGE_SIZE + PAGE_SIZE
    f32 = jnp.float32
    half = MLA_ROPE // 2
    w = w_ukv.reshape(KV_LORA, MLA_HEADS, MLA_NOPE + MLA_V)
    w_uk = w[:, :, :MLA_NOPE].reshape(KV_LORA, MLA_HEADS * MLA_NOPE).T.astype(jnp.bfloat16)
    w_uv = w[:, :, MLA_NOPE:].transpose(1, 0, 2).astype(jnp.bfloat16)
    qt = q_m.transpose(0, 2, 1, 3)
    eye_h = jnp.eye(MLA_HEADS, dtype=f32)
    q_blk = (qt[:, :, :, None, :MLA_NOPE] * eye_h[None, :, None, :, None]).reshape(
        DB, MLA_ROWS_Q, MLA_HEADS * MLA_NOPE)
    q1, q2 = qt[..., MLA_NOPE:MLA_NOPE + half], qt[..., MLA_NOPE + half:]
    q_pe = jnp.concatenate([q1, q2, q2, -q1], axis=-1).reshape(DB, MLA_ROWS_Q, 2 * MLA_ROPE)
    gn = jnp.tile(k_gain[:MLA_NOPE].astype(f32), MLA_HEADS).reshape(1, MLA_HEADS * MLA_NOPE)
    g1, g2 = k_gain[MLA_NOPE:MLA_NOPE + half].astype(f32), k_gain[MLA_NOPE + half:].astype(f32)
    c, s = cos[:n_pos].T, sin[:n_pos].T
    tbl = jnp.concatenate([c * g1[:, None], c * g2[:, None], s * g1[:, None], s * g2[:, None]], axis=0)
    eye2 = jnp.concatenate([jnp.eye(MLA_ROPE, dtype=f32)] * 2, axis=0).astype(jnp.bfloat16)
    new_pad = jnp.pad(rows_new, ((0, 0), (0, PAGE_SIZE - S), (0, 0)))
    lc = MLA_CHUNK_PAGES * PAGE_SIZE
    const = lambda shape: pl.BlockSpec(shape, lambda b, pt: (0,) * len(shape))
    out = pl.pallas_call(
        functools.partial(_mla_sample_kernel, layer, n_pages),
        grid_spec=pltpu.PrefetchScalarGridSpec(
            num_scalar_prefetch=1,
            grid=(DB,),
            in_specs=[pl.BlockSpec(memory_space=pl.ANY),
                      pl.BlockSpec((1, PAGE_SIZE, MLA_CACHE), lambda b, pt: (b, 0, 0)),
                      pl.BlockSpec((1, MLA_ROWS_Q, MLA_HEADS * MLA_NOPE), lambda b, pt: (b, 0, 0)),
                      pl.BlockSpec((1, MLA_ROWS_Q, 2 * MLA_ROPE), lambda b, pt: (b, 0, 0)),
                      const((1, MLA_HEADS * MLA_NOPE)),
                      const((MLA_HEADS * MLA_NOPE, KV_LORA)),
                      const((MLA_HEADS, KV_LORA, MLA_V)),
                      const((2 * MLA_ROPE, MLA_ROPE)),
                      const((2 * MLA_ROPE, n_pos))],
            out_specs=pl.BlockSpec((1, MLA_ROWS_Q, MLA_V), lambda b, pt: (b, 0, 0)),
            scratch_shapes=[pltpu.VMEM((2, lc, MLA_CACHE), f32),
                            pltpu.SemaphoreType.DMA((2,)),
                            pltpu.VMEM((MLA_ROWS_Q, 1), f32),
                            pltpu.VMEM((MLA_ROWS_Q, 1), f32),
                            pltpu.VMEM((MLA_ROWS_Q, KV_LORA), f32)]),
        out_shape=jax.ShapeDtypeStruct((DB, MLA_ROWS_Q, MLA_V), f32),
        compiler_params=pltpu.CompilerParams(dimension_semantics=("arbitrary",), vmem_limit_bytes=VMEM_LIMIT_BYTES),
        name="mla_sample_attention",
    )(page_table, cache_mla, new_pad, q_blk, q_pe, gn, w_uk, w_uv, eye2, tbl)
    return out.reshape(DB, MLA_HEADS, S, MLA_V).transpose(0, 2, 1, 3)


def _nsa_sample_kernel(layer, n_pages, pt_ref, ccmp_ref, csel_ref, q_ref, ncmp_ref, nsel_ref, nwin_ref, wbuf_ref,
                       gain_ref, bcmp_ref, bfar_ref, blast_ref, bnew_ref, bwin_ref, expand_ref,
                       ocmp_ref, osel_ref, owin_ref, buf_ref, sem_ref, sum_ref, msel_ref, m_ref, l_ref, acc_ref):
    b = pl.program_id(0)
    n_seq = pl.num_programs(0)
    cp = NSA_CHUNK_PAGES
    lc = cp * PAGE_SIZE
    n_chunks = n_pages // cp
    bpc = lc // NSA_BLOCK
    n_past = n_pages * PAGE_SIZE
    nbp = n_past // NSA_BLOCK
    nbs = sum_ref.shape[0]
    scale = NSA_DIM ** -0.5

    def page_copy(cache_ref, seq, chunk, p, slot):
        page = pt_ref[seq, chunk * cp + p]
        return pltpu.make_async_copy(cache_ref.at[layer, page], buf_ref.at[slot, pl.ds(p * PAGE_SIZE, PAGE_SIZE)],
                                     sem_ref.at[slot])

    def start_chunk(cache_ref, seq, chunk, slot):
        for p in range(cp):
            page_copy(cache_ref, seq, chunk, p, slot).start()

    def wait_chunk(cache_ref, seq, chunk, slot):
        for p in range(cp):
            page_copy(cache_ref, seq, chunk, p, slot).wait()

    @pl.when(b == 0)
    def _():
        start_chunk(ccmp_ref, 0, 0, 0)

    def cmp_body(c, carry):
        slot = c % 2

        @pl.when(c + 1 < n_chunks)
        def _():
            start_chunk(ccmp_ref, b, c + 1, 1 - slot)

        @pl.when(c + 1 == n_chunks)
        def _():
            start_chunk(csel_ref, b, 0, 1 - slot)

        wait_chunk(ccmp_ref, b, c, slot)
        rows = buf_ref[slot].reshape(bpc, NSA_BLOCK, KV_LANES)
        sum_ref[pl.ds(pl.multiple_of(c * bpc, bpc), bpc), :] = jnp.sum(rows, axis=1) * (1.0 / NSA_BLOCK)
        return carry

    lax.fori_loop(0, n_chunks, cmp_body, 0)
    pad_rows = nbs - nbp
    new_mean = jnp.sum(ncmp_ref[0], axis=0, keepdims=True) * (1.0 / NSA_BLOCK)
    row0 = lax.broadcasted_iota(jnp.int32, (pad_rows, KV_LANES), 0) == 0
    sum_ref[nbp:, :] = jnp.where(row0, jnp.broadcast_to(new_mean, (pad_rows, KV_LANES)), 0.0)

    summ = sum_ref[...]
    km, vm = summ[:, :LANES], summ[:, LANES:]
    lo = lax.broadcasted_iota(jnp.int32, km.shape, 1) < NSA_DIM
    sq = km * km
    ss = jnp.where(lo, jnp.sum(jnp.where(lo, sq, 0.0), axis=-1, keepdims=True),
                   jnp.sum(jnp.where(lo, 0.0, sq), axis=-1, keepdims=True))
    kc = km * lax.rsqrt(ss * (1.0 / NSA_DIM) + EPS) * gain_ref[...]
    vcb = vm.astype(jnp.bfloat16)

    tt = lax.broadcasted_iota(jnp.int32, (ROWS_Q, nbs), 0) % DEC_SEQ
    nn = lax.broadcasted_iota(jnp.int32, (ROWS_Q, nbs), 1)
    cmask = (n_past + tt) >= (nn * NSA_BLOCK + NSA_BLOCK - 1)
    cmaskf = cmask.astype(jnp.float32)
    o_cmp, imps = [], []
    for g in range(NSA_GROUPS):
        s = lax.dot_general(q_ref[0, g], kc, (((1,), (1,)), ((), ())),
                            precision=lax.Precision.HIGHEST, preferred_element_type=jnp.float32)
        s = jnp.where(cmask, s * scale + bcmp_ref[g], NEG)
        p = jnp.exp(s - jnp.max(s, axis=-1, keepdims=True)) * cmaskf
        pr = p / jnp.maximum(jnp.sum(p, axis=-1, keepdims=True), 1e-30)
        o_cmp.append(jnp.dot(pr.astype(jnp.bfloat16), vcb, preferred_element_type=jnp.float32))
        imp = pr[0:DEC_SEQ]
        for r in range(1, NSA_REP):
            imp = imp + pr[r * DEC_SEQ:(r + 1) * DEC_SEQ]
        imps.append(imp)
    imp = jnp.concatenate(imps, axis=0)
    n16 = lax.broadcasted_iota(jnp.int32, imp.shape, 1)
    t16 = lax.broadcasted_iota(jnp.int32, imp.shape, 0) % DEC_SEQ
    cur = (n_past + t16) // NSA_BLOCK
    forced = (n16 == cur) | (n16 == 0)
    score = jnp.where(n16 <= cur, jnp.where(forced, -NEG, imp), NEG)
    cnt = jnp.zeros(imp.shape, jnp.float32)
    for j in range(nbp + 1):
        col = score[:, j:j + 1]
        beats = (col > score) | ((col == score) & (n16 > j))
        cnt = cnt + beats.astype(jnp.float32)
    msel = (cnt < float(NSA_TOPN)).astype(jnp.float32)
    msel_ref[...] = jnp.zeros(msel_ref.shape, jnp.float32)
    for c in range(n_chunks):
        msel_ref[c, :, 0:bpc] = msel[:, c * bpc:(c + 1) * bpc]
    msel_new = msel[:, nbp:nbp + 1]

    def attend(g, kv, bias, mask):
        qb = q_ref[0, g].astype(jnp.bfloat16)
        k = kv[:, :LANES].astype(jnp.bfloat16)
        v = kv[:, LANES:].astype(jnp.bfloat16)
        s = lax.dot_general(qb, k, (((1,), (1,)), ((), ())), preferred_element_type=jnp.float32)
        s = jnp.where(mask, s * scale + bias, NEG)
        m_old = m_ref[g]
        m_new = jnp.maximum(m_old, jnp.max(s, axis=-1, keepdims=True))
        p = jnp.exp(s - m_new) * mask.astype(jnp.float32)
        alpha = jnp.exp(m_old - m_new)
        l_ref[g] = alpha * l_ref[g] + jnp.sum(p, axis=-1, keepdims=True)
        m_ref[g] = m_new
        acc_ref[g] = alpha * acc_ref[g] + jnp.dot(p.astype(jnp.bfloat16), v, preferred_element_type=jnp.float32)

    def reset():
        m_ref[...] = jnp.full(m_ref.shape, NEG, jnp.float32)
        l_ref[...] = jnp.zeros(l_ref.shape, jnp.float32)
        acc_ref[...] = jnp.zeros(acc_ref.shape, jnp.float32)

    def result(g):
        return acc_ref[g] / jnp.maximum(l_ref[g], 1e-30)

    tile6 = lambda a: jnp.concatenate([a] * NSA_REP, axis=0)
    tn = lax.broadcasted_iota(jnp.int32, (ROWS_Q, PAGE_SIZE), 0) % DEC_SEQ
    jn = lax.broadcasted_iota(jnp.int32, (ROWS_Q, PAGE_SIZE), 1)
    new_causal = jn <= tn

    reset()

    def sel_body(c, carry):
        slot = c % 2

        @pl.when(c + 1 < n_chunks)
        def _():
            start_chunk(csel_ref, b, c + 1, 1 - slot)

        @pl.when((c + 1 == n_chunks) & (b + 1 < n_seq))
        def _():
            start_chunk(ccmp_ref, b + 1, 0, 1 - slot)

        wait_chunk(csel_ref, b, c, slot)
        kv = buf_ref[slot]
        mexp = jnp.dot(msel_ref[c].astype(jnp.bfloat16), expand_ref[...], preferred_element_type=jnp.float32)
        last = jnp.where(c == n_chunks - 1, 1.0, 0.0).astype(jnp.float32)
        for g in range(NSA_GROUPS):
            bias = bfar_ref[g] + last * blast_ref[g]
            attend(g, kv, bias, tile6(mexp[g * DEC_SEQ:(g + 1) * DEC_SEQ]) > 0.5)
        return carry

    lax.fori_loop(0, n_chunks, sel_body, 0)
    o_sel = []
    for g in range(NSA_GROUPS):
        attend(g, nsel_ref[0], bnew_ref[g], new_causal & (tile6(msel_new[g * DEC_SEQ:(g + 1) * DEC_SEQ]) > 0.5))
        o_sel.append(result(g))

    reset()
    n_buf = wbuf_ref.shape[2]
    tw = lax.broadcasted_iota(jnp.int32, (ROWS_Q, n_buf), 0) % DEC_SEQ
    cw = lax.broadcasted_iota(jnp.int32, (ROWS_Q, n_buf), 1)
    dist_w = n_buf + tw - cw
    wmask = (dist_w >= 0) & (dist_w < WINDOW)
    o_win = []
    for g in range(NSA_GROUPS):
        attend(g, wbuf_ref[0, 0], bwin_ref[g], wmask)
        attend(g, nwin_ref[0], bnew_ref[g], new_causal)
        o_win.append(result(g))

    lane_lo = lax.broadcasted_iota(jnp.int32, (ROWS_Q, LANES), 1) < NSA_DIM
    ocmp_ref[0] = jnp.where(lane_lo, o_cmp[0], o_cmp[1])
    osel_ref[0] = jnp.where(lane_lo, o_sel[0], o_sel[1])
    owin_ref[0] = jnp.where(lane_lo, o_win[0], o_win[1])


def _nsa_sample_tables(rel_bias, n_pages, n_buf):
    n_past = n_pages * PAGE_SIZE
    nbp = n_past // NSA_BLOCK
    nbs = -(-(nbp + 1) // LANES) * LANES
    lc = NSA_CHUNK_PAGES * PAGE_SIZE
    bt = rel_bias.astype(jnp.float32).reshape(N_BUCKETS, NSA_GROUPS, NSA_REP)
    t = jnp.arange(DEC_SEQ)

    def table(dist):
        tb = bt[_rel_bucket(dist)]
        return tb.transpose(2, 3, 0, 1).reshape(NSA_GROUPS, ROWS_Q, dist.shape[1])

    bcmp = table(n_past + t[:, None] - (jnp.arange(nbs)[None, :] * NSA_BLOCK + NSA_BLOCK - 1))
    bfar = jnp.broadcast_to(bt[N_BUCKETS - 1][:, :, None, None], (NSA_GROUPS, NSA_REP, DEC_SEQ, 1)).reshape(
        NSA_GROUPS, ROWS_Q, 1)
    blast = table(n_past + t[:, None] - (n_past - lc + jnp.arange(lc))[None, :]) - bfar
    bnew = table(t[:, None] - jnp.arange(PAGE_SIZE)[None, :])
    bwin = table(n_buf + t[:, None] - jnp.arange(n_buf)[None, :])
    expand = (jnp.arange(lc)[None, :] // NSA_BLOCK == jnp.arange(LANES)[:, None]).astype(jnp.bfloat16)
    return bcmp, bfar, blast, bnew, bwin, expand


def _nsa_sample(q_n, kc_new, vc_new, ks_new, vs_new, kw_new, vw_new, page_table, cache_cmp, cache_sel, win_state,
                layer, kc_gain, tables):
    DB, S = q_n.shape[:2]
    n_pages = page_table.shape[1]
    n_buf = win_state.shape[2]
    lc = NSA_CHUNK_PAGES * PAGE_SIZE
    n_chunks = n_pages // NSA_CHUNK_PAGES
    assert S == DEC_SEQ and n_pages % NSA_CHUNK_PAGES == 0 and n_chunks % 2 == 0 and n_buf == WINDOW
    assert lc + 1 >= MAX_DISTANCE and lc // NSA_BLOCK <= LANES
    f32 = jnp.float32
    bcmp, bfar, blast, bnew, bwin, expand = tables
    nbs = bcmp.shape[-1]
    qt = q_n.transpose(0, 2, 3, 1, 4).reshape(DB, NSA_GROUPS, ROWS_Q, NSA_DIM)
    q_pad = _group_padded_queries(qt)
    padrows = lambda a: jnp.pad(a, ((0, 0), (0, PAGE_SIZE - S), (0, 0)))
    new_cmp = jnp.concatenate([kc_new, vc_new], axis=-1)
    new_sel = padrows(jnp.concatenate([ks_new, vs_new], axis=-1))
    new_win = padrows(jnp.concatenate([kw_new, vw_new], axis=-1))
    gain = jnp.tile(kc_gain.astype(f32), NSA_GROUPS).reshape(1, LANES)
    const = lambda shape: pl.BlockSpec(shape, lambda b, pt: (0,) * len(shape))
    per_seq = lambda shape: pl.BlockSpec((1,) + shape, lambda b, pt: (b,) + (0,) * len(shape))
    o_shape = jax.ShapeDtypeStruct((DB, ROWS_Q, LANES), f32)
    outs = pl.pallas_call(
        functools.partial(_nsa_sample_kernel, layer, n_pages),
        grid_spec=pltpu.PrefetchScalarGridSpec(
            num_scalar_prefetch=1,
            grid=(DB,),
            in_specs=[pl.BlockSpec(memory_space=pl.ANY), pl.BlockSpec(memory_space=pl.ANY),
                      per_seq((NSA_GROUPS, ROWS_Q, LANES)),
                      per_seq((S, KV_LANES)), per_seq((PAGE_SIZE, KV_LANES)), per_seq((PAGE_SIZE, KV_LANES)),
                      pl.BlockSpec((1, 1, n_buf, KV_LANES), lambda b, pt: (layer, b, 0, 0)),
                      const((1, LANES)),
                      const((NSA_GROUPS, ROWS_Q, nbs)), const((NSA_GROUPS, ROWS_Q, 1)),
                      const((NSA_GROUPS, ROWS_Q, lc)), const((NSA_GROUPS, ROWS_Q, PAGE_SIZE)),
                      const((NSA_GROUPS, ROWS_Q, n_buf)), const((LANES, lc))],
            out_specs=[per_seq((ROWS_Q, LANES))] * 3,
            scratch_shapes=[pltpu.VMEM((2, lc, KV_LANES), f32),
                            pltpu.SemaphoreType.DMA((2,)),
                            pltpu.VMEM((nbs, KV_LANES), f32),
                            pltpu.VMEM((n_chunks, NSA_GROUPS * DEC_SEQ, LANES), f32),
                            pltpu.VMEM((NSA_GROUPS, ROWS_Q, 1), f32),
                            pltpu.VMEM((NSA_GROUPS, ROWS_Q, 1), f32),
                            pltpu.VMEM((NSA_GROUPS, ROWS_Q, LANES), f32)]),
        out_shape=[o_shape] * 3,
        compiler_params=pltpu.CompilerParams(dimension_semantics=("arbitrary",), vmem_limit_bytes=VMEM_LIMIT_BYTES),
        name="nsa_sample_attention",
    )(page_table, cache_cmp, cache_sel, q_pad, new_cmp, new_sel, new_win, win_state, gain,
      bcmp, bfar, blast, bnew, bwin, expand)
    back = lambda o: o.reshape(DB, NSA_REP, S, NSA_GROUPS, NSA_DIM).transpose(0, 2, 3, 1, 4)
    return tuple(back(o) for o in outs)


def _pool_mix(u_ext, pos_ext, n_new, w_pool, scale):
    N, L = u_ext.shape[:2]
    uf = u_ext.astype(jnp.float32)
    cs = jnp.concatenate([jnp.zeros_like(uf[:, :1]), jnp.cumsum(uf, axis=1)], axis=1)
    e = jnp.arange(L - n_new, L)
    parts = []
    for gi, w in enumerate(POOL_WINDOWS):
        c = slice(gi * POOL_GROUP, (gi + 1) * POOL_GROUP)
        win_sum = cs[:, e + 1, c] - cs[:, jnp.maximum(e + 1 - w, 0), c]
        cnt = jnp.minimum(w, pos_ext[e] + 1).astype(jnp.float32)[None, :, None]
        parts.append(win_sum / cnt - uf[:, e, c])
    d = jnp.stack(parts, axis=2).astype(u_ext.dtype)
    y = jnp.einsum('ntgc,gcd->ntgd', d, w_pool)
    return y.reshape(N, n_new, POOL_WIDTH) * scale


def _project(x, pos, p):
    z = _mm(_rmsnorm(x, p['attn_norm']), p['w_in'], tn=384)
    (u_pool, q_lat, kv_lat, k_pe, q_nsa, k_cmp, v_cmp, k_sel, v_sel, k_win, v_win, g_logit) = _split_in(z)
    N, L = x.shape[:2]
    grp = lambda a: a.reshape(N, L, NSA_GROUPS, NSA_DIM)
    q_mla = _mla_q(q_lat, p, pos)
    mla_rows = jnp.concatenate([_rmsnorm(kv_lat, p['mla_kv_norm']), k_pe], axis=-1)
    q_n = _rmsnorm(q_nsa.reshape(N, L, NSA_GROUPS, NSA_REP, NSA_DIM), p['nsa_q_gain'])
    k_sel = _rmsnorm(grp(k_sel), p['nsa_k_gain'][1])
    k_win = _rmsnorm(grp(k_win), p['nsa_k_gain'][2])
    gates = jax.nn.sigmoid(g_logit).reshape(N, L, 3, NSA_GROUPS, NSA_REP)
    return u_pool, q_mla, mla_rows, q_n, grp(k_cmp), grp(v_cmp), k_sel, grp(v_sel), k_win, grp(v_win), gates


def _finish(x, pool_o, mla_o, nsa_o, p):
    N, L = x.shape[:2]
    mix = jnp.concatenate([pool_o, mla_o.reshape(N, L, MLA_WIDTH), nsa_o.reshape(N, L, NSA_WIDTH)], axis=-1)
    x = x + _mm(mix, p['w_out'])
    h = _rmsnorm(x, p['ffn_norm'])
    a = jax.nn.silu(_mm(h, p['w_gate'])) * _mm(h, p['w_up'])
    return x + _mm(a, p['w_down'])


def kernel(x_prompt, x_sample, cache_mla, cache_nsa_cmp, cache_nsa_sel, state_nsa_win, state_pool, page_table,
           rel_bias, attn_norm, w_in, pool_w, pool_scale, mla_q_norm, mla_kv_norm, w_uq, w_ukv,
           mla_q_gain, mla_k_gain, nsa_q_gain, nsa_k_gain, w_out, ffn_norm, w_gate, w_up, w_down):
    xp, xs = x_prompt, x_sample
    B, T = xp.shape[:2]
    DB, S = xs.shape[:2]
    n_pages = page_table.shape[1]
    n_buf = state_nsa_win.shape[2]
    pos_p = jnp.arange(T)
    pos_s = PAST_LEN + jnp.arange(S)
    cos, sin = _rope_tables(n_pages * PAGE_SIZE + PAGE_SIZE)
    prompt_tables = _nsa_prompt_tables(rel_bias, T)
    sample_tables = _nsa_sample_tables(rel_bias, n_pages, n_buf)
    cache_cmp2 = cache_nsa_cmp.reshape(cache_nsa_cmp.shape[:3] + (KV_LANES,))
    cache_sel2 = cache_nsa_sel.reshape(cache_nsa_sel.shape[:3] + (KV_LANES,))
    win_state2 = state_nsa_win.reshape(state_nsa_win.shape[:3] + (KV_LANES,))
    flat = lambda a: a.reshape(a.shape[:2] + (NSA_KV,))
    nmla_p, ncmp_p, nsel_p, nwin_p, npool_p = [], [], [], [], []
    nmla_s, ncmp_s, nsel_s, nwin_s, npool_s = [], [], [], [], []
    for l in range(DEPTH):
        p = {'attn_norm': attn_norm[l], 'w_in': w_in[l], 'mla_q_norm': mla_q_norm[l],
             'mla_kv_norm': mla_kv_norm[l], 'w_uq': w_uq[l], 'w_ukv': w_ukv[l],
             'mla_q_gain': mla_q_gain[l], 'mla_k_gain': mla_k_gain[l], 'nsa_q_gain': nsa_q_gain[l],
             'nsa_k_gain': nsa_k_gain[l], 'w_out': w_out[l], 'ffn_norm': ffn_norm[l],
             'w_gate': w_gate[l], 'w_up': w_up[l], 'w_down': w_down[l]}

        u, q_m, rows_m, q_n, kc, vc, ks, vs, kw, vw, gates = _project(xp, pos_p, p)
        pool_o = _pool_mix(u, pos_p, T, pool_w[l], pool_scale[l])
        k_m, v_m = _mla_kv(rows_m, p, pos_p)
        mla_o = jax.vmap(_mla_prompt_seq)(q_m, k_m, v_m)
        o_cmp, o_sel, o_win = _nsa_prompt(q_n, flat(kc), flat(vc), flat(ks), flat(vs), flat(kw), flat(vw),
                                          p['nsa_k_gain'][0], prompt_tables)
        nsa_o = _nsa_combine(gates, o_cmp, o_sel, o_win)
        xp = _finish(xp, pool_o, mla_o, nsa_o, p)
        nmla_p.append(rows_m)
        ncmp_p.append(jnp.stack([kc, vc], axis=2))
        nsel_p.append(jnp.stack([ks, vs], axis=2))
        nwin_p.append(jnp.stack([kw, vw], axis=2)[:, T - min(WINDOW, T):])
        npool_p.append(u[:, T - POOL_STATE:])

        u, q_m, rows_m, q_n, kc, vc, ks, vs, kw, vw, gates = _project(xs, pos_s, p)
        u_ext = jnp.concatenate([state_pool[l], u], axis=1)
        n_ctx = state_pool.shape[2]
        pool_o = _pool_mix(u_ext, PAST_LEN - n_ctx + jnp.arange(n_ctx + S), S, pool_w[l], pool_scale[l])
        mla_o = _mla_sample(q_m, rows_m, page_table, cache_mla, l, p['w_ukv'], p['mla_k_gain'], cos, sin)
        o_cmp, o_sel, o_win = _nsa_sample(q_n, flat(kc), flat(vc), flat(ks), flat(vs), flat(kw), flat(vw),
                                          page_table, cache_cmp2, cache_sel2, win_state2, l,
                                          p['nsa_k_gain'][0], sample_tables)
        nsa_o = _nsa_combine(gates, o_cmp, o_sel, o_win)
        xs = _finish(xs, pool_o, mla_o, nsa_o, p)
        nmla_s.append(rows_m)
        ncmp_s.append(jnp.stack([kc, vc], axis=2))
        nsel_s.append(jnp.stack([ks, vs], axis=2))
        nwin_s.append(jnp.concatenate([state_nsa_win[l], jnp.stack([kw, vw], axis=2)], axis=1)[:, S:S + n_buf])
        npool_s.append(u_ext[:, S:S + n_ctx])

    return (xp, xs,
            jnp.stack(nmla_p), jnp.stack(ncmp_p), jnp.stack(nsel_p), jnp.stack(nwin_p), jnp.stack(npool_p),
            jnp.stack(nmla_s), jnp.stack(ncmp_s), jnp.stack(nsel_s), jnp.stack(nwin_s), jnp.stack(npool_s))
```

```python
import functools
import math

import jax
import jax.numpy as jnp
import numpy as np
from jax import lax
from jax.experimental import pallas as pl
from jax.experimental.pallas import tpu as pltpu

D_MODEL = 2048
BATCH = 2
SEQ = 4096
DEPTH = 2
DEC_BATCH = 128
DEC_SEQ = 8
PAST_LEN = 16384
PAGE_SIZE = 128
POOL_WINDOWS = (2, 4, 8, 16)
POOL_WIDTH = D_MODEL // 4
POOL_GROUP = POOL_WIDTH // len(POOL_WINDOWS)
POOL_STATE = max(POOL_WINDOWS) - 1
MLA_HEADS = 6
MLA_NOPE = 64
MLA_ROPE = 32
MLA_QK = MLA_NOPE + MLA_ROPE
MLA_V = 128
MLA_WIDTH = MLA_HEADS * MLA_V
Q_LORA = 384
KV_LORA = 128
MLA_CACHE = KV_LORA + MLA_ROPE
ROPE_THETA = 10000.0
NSA_HEADS = 12
NSA_DIM = 64
NSA_GROUPS = 2
NSA_REP = NSA_HEADS // NSA_GROUPS
NSA_WIDTH = NSA_HEADS * NSA_DIM
NSA_KV = NSA_GROUPS * NSA_DIM
NSA_BLOCK = 64
NSA_TOPN = 16
WINDOW = 512
MIX_WIDTH = POOL_WIDTH + MLA_WIDTH + NSA_WIDTH
IN_SPLITS = (POOL_WIDTH, Q_LORA, KV_LORA, MLA_ROPE, NSA_WIDTH) + (NSA_KV,) * 6 + (3 * NSA_HEADS,)
IN_COLS = sum(IN_SPLITS)
N_BUCKETS = 32
MAX_DISTANCE = 1024
D_FF = -(-8 * D_MODEL // (3 * 256)) * 256
Q_BLOCK = 128
EPS = 1e-6
NEG = -1e30

LANES = 128
KV_LANES = 2 * NSA_KV
POOL_TILE = 512
POOL_HALO = 16
VMEM_LIMIT_BYTES = 48 * 1024 * 1024
QT = 128
SEL_KT = 512
N_DELTA = 9
MLA_CHUNK_PAGES = 16
NSA_CHUNK_PAGES = 16
ROWS_Q = NSA_REP * DEC_SEQ
MLA_ROWS_Q = MLA_HEADS * DEC_SEQ


def _mm_kernel(x_ref, w_ref, o_ref):
    o_ref[...] = jnp.dot(x_ref[...], w_ref[...], preferred_element_type=jnp.float32).astype(o_ref.dtype)


def _pmm(x, w, tm=512, tn=512, out_dtype=jnp.float32):
    M, K = x.shape
    N = w.shape[1]
    tm = min(tm, M)
    assert M % tm == 0
    n_pad = -(-N // tn) * tn
    xb = x.astype(jnp.bfloat16)
    wb = w.astype(jnp.bfloat16)
    if n_pad != N:
        wb = jnp.pad(wb, ((0, 0), (0, n_pad - N)))
    out = pl.pallas_call(
        _mm_kernel,
        grid=(M // tm, n_pad // tn),
        in_specs=[pl.BlockSpec((tm, K), lambda i, j: (i, 0)),
                  pl.BlockSpec((K, tn), lambda i, j: (0, j))],
        out_specs=pl.BlockSpec((tm, tn), lambda i, j: (i, j)),
        out_shape=jax.ShapeDtypeStruct((M, n_pad), out_dtype),
        compiler_params=pltpu.CompilerParams(
            dimension_semantics=("parallel", "parallel"), vmem_limit_bytes=VMEM_LIMIT_BYTES),
        name="dense_mm",
    )(xb, wb)
    return out[:, :N] if n_pad != N else out


def _mm(x, w, **kw):
    lead = x.shape[:-1]
    return _pmm(x.reshape(-1, x.shape[-1]), w, **kw).reshape(lead + (w.shape[1],))


def _rmsnorm(x, g):
    xf = x.astype(jnp.float32)
    y = xf * lax.rsqrt(jnp.mean(xf * xf, axis=-1, keepdims=True) + EPS)
    return (y * g.astype(jnp.float32)).astype(x.dtype)


def _masked_softmax(s, mask):
    s = jnp.where(mask, s.astype(jnp.float32), NEG)
    p = jnp.exp(s - jnp.max(s, axis=-1, keepdims=True)) * mask
    return p / jnp.maximum(jnp.sum(p, axis=-1, keepdims=True), 1e-30)


def _rope_tables(n_pos):
    half = MLA_ROPE // 2
    inv = ROPE_THETA ** (-jnp.arange(half, dtype=jnp.float32) / half)
    ang = jnp.arange(n_pos).astype(jnp.float32)[:, None] * inv
    return jnp.cos(ang), jnp.sin(ang)


def _rope(x, pos):
    half = x.shape[-1] // 2
    inv = ROPE_THETA ** (-jnp.arange(half, dtype=jnp.float32) / half)
    ang = pos.astype(jnp.float32)[:, None] * inv
    cos, sin = jnp.cos(ang)[:, None, :], jnp.sin(ang)[:, None, :]
    xf = x.astype(jnp.float32)
    x1, x2 = xf[..., :half], xf[..., half:]
    return jnp.concatenate([x1 * cos - x2 * sin, x1 * sin + x2 * cos], axis=-1).astype(x.dtype)


def _rel_bucket(dist):
    n = jnp.maximum(dist, 0)
    exact = N_BUCKETS // 2
    nf = jnp.maximum(n, exact).astype(jnp.float32)
    large = exact + (jnp.log(nf / exact) / math.log(MAX_DISTANCE / exact) * (N_BUCKETS - exact)).astype(jnp.int32)
    return jnp.where(n < exact, n, jnp.minimum(large, N_BUCKETS - 1))


def _split_in(z):
    cuts = [int(c) for c in np.cumsum(IN_SPLITS)[:-1]]
    return jnp.split(z, cuts, axis=-1)


def _mla_q(q_lat, p, pos):
    q = _mm(_rmsnorm(q_lat, p['mla_q_norm']), p['w_uq'])
    q = _rmsnorm(q.reshape(q.shape[:-1] + (MLA_HEADS, MLA_QK)), p['mla_q_gain'])
    return jnp.concatenate([q[..., :MLA_NOPE], _rope(q[..., MLA_NOPE:], pos)], axis=-1)


def _mla_kv(rows, p, pos):
    kv = (rows[..., :KV_LORA] @ p['w_ukv']).reshape(rows.shape[:-1] + (MLA_HEADS, MLA_NOPE + MLA_V))
    k_pe = jnp.broadcast_to(rows[..., None, KV_LORA:], kv.shape[:-1] + (MLA_ROPE,))
    k = _rmsnorm(jnp.concatenate([kv[..., :MLA_NOPE], k_pe], axis=-1), p['mla_k_gain'])
    k = jnp.concatenate([k[..., :MLA_NOPE], _rope(k[..., MLA_NOPE:], pos)], axis=-1)
    return k, kv[..., MLA_NOPE:]


def _mha(q, k, v, mask):
    s = jnp.einsum('thd,lhd->htl', q, k).astype(jnp.float32) * q.shape[-1] ** -0.5
    pr = _masked_softmax(s, mask[None])
    return jnp.einsum('htl,lhd->thd', pr.astype(v.dtype), v)


def _mla_prompt_seq(q, k, v):
    T = q.shape[0]
    k_pos = jnp.arange(T)

    def q_block(q0):
        qb = lax.dynamic_slice_in_dim(q, q0, Q_BLOCK)
        return _mha(qb, k, v, k_pos[None, :] <= (q0 + jnp.arange(Q_BLOCK))[:, None])

    return lax.map(q_block, jnp.arange(0, T, Q_BLOCK)).reshape(T, MLA_HEADS, MLA_V)


def _nsa_combine(gates, o_cmp, o_sel, o_win):
    g = gates[..., None]
    return g[:, :, 0] * o_cmp + g[:, :, 1] * o_sel + g[:, :, 2] * o_win


def _blocksum_kernel(k_ref, v_ref, gain_ref, kc_ref, vc_ref):
    nb = kc_ref.shape[1]
    k = k_ref[0].reshape(nb, NSA_BLOCK, LANES)
    v = v_ref[0].reshape(nb, NSA_BLOCK, LANES)
    km = jnp.sum(k, axis=1) * (1.0 / NSA_BLOCK)
    vm = jnp.sum(v, axis=1) * (1.0 / NSA_BLOCK)
    lo = lax.broadcasted_iota(jnp.int32, km.shape, 1) < NSA_DIM
    sq = km * km
    ss_lo = jnp.sum(jnp.where(lo, sq, 0.0), axis=-1, keepdims=True)
    ss_hi = jnp.sum(jnp.where(lo, 0.0, sq), axis=-1, keepdims=True)
    ss = jnp.where(lo, ss_lo, ss_hi)
    kc_ref[0] = km * lax.rsqrt(ss * (1.0 / NSA_DIM) + EPS) * gain_ref[...]
    vc_ref[0] = vm


def _block_summaries(k_cmp, v_cmp, kc_gain):
    B, T, _ = k_cmp.shape
    nb = T // NSA_BLOCK
    gain = jnp.tile(kc_gain.astype(jnp.float32), NSA_GROUPS).reshape(1, LANES)
    spec = pl.BlockSpec((1, T, LANES), lambda b: (b, 0, 0))
    ospec = pl.BlockSpec((1, nb, LANES), lambda b: (b, 0, 0))
    return pl.pallas_call(
        _blocksum_kernel,
        grid=(B,),
        in_specs=[spec, spec, pl.BlockSpec((1, LANES), lambda b: (0, 0))],
        out_specs=[ospec, ospec],
        out_shape=[jax.ShapeDtypeStruct((B, nb, LANES), jnp.float32)] * 2,
        compiler_params=pltpu.CompilerParams(dimension_semantics=("parallel",), vmem_limit_bytes=VMEM_LIMIT_BYTES),
        name="nsa_block_summaries",
    )(k_cmp, v_cmp, gain)


def _nsa_prompt_kernel(q_ref, kc_ref, vc_ref, bcmp_ref, ks_ref, vs_ref, kw_ref, vw_ref, btile_ref,
                       ocmp_ref, osel_ref, owin_ref, mexp_ref, m_ref, l_ref, acc_ref):
    qi = pl.program_id(1)
    T = ks_ref.shape[1]
    nb = kc_ref.shape[1]
    n_kt = T // QT
    scale = NSA_DIM ** -0.5
    rows = NSA_REP * QT

    tq = lax.broadcasted_iota(jnp.int32, (QT, nb), 0) + qi * QT
    nidx = lax.broadcasted_iota(jnp.int32, (QT, nb), 1)
    ii = lax.broadcasted_iota(jnp.int32, (QT, QT), 0)
    jj = lax.broadcasted_iota(jnp.int32, (QT, QT), 1)
    expand = (lax.broadcasted_iota(jnp.int32, (nb, T), 1) // NSA_BLOCK
              == lax.broadcasted_iota(jnp.int32, (nb, T), 0)).astype(jnp.bfloat16)
    lane_lo = lax.broadcasted_iota(jnp.int32, (QT, LANES), 1) < NSA_DIM

    def block_mask_bias(g, kb, windowed):
        delta = qi - kb
        dist = delta * QT + ii - jj
        if windowed:
            mask = (dist >= 0) & (dist < WINDOW)
        else:
            mask = (dist >= 0) & (mexp_ref[g, kb] > 0.5)
        dcl = jnp.clip(delta, 0, N_DELTA - 1)
        return mask, [btile_ref[g, r, dcl] for r in range(NSA_REP)]

    def tile_inputs(g, kb0, n_blk, windowed):
        parts = [block_mask_bias(g, kb0 + j, windowed) for j in range(n_blk)]
        mask = jnp.concatenate([p[0] for p in parts], axis=1)
        biases = [jnp.concatenate([p[1][r] for p in parts], axis=1) for r in range(NSA_REP)]
        return mask, biases

    def flash_sel(g, qb):
        m_ref[...] = jnp.full(m_ref.shape, NEG, jnp.float32)
        l_ref[...] = jnp.zeros(l_ref.shape, jnp.float32)
        acc_ref[...] = jnp.zeros(acc_ref.shape, jnp.float32)
        n_blk = SEL_KT // QT

        def body(kt, carry):
            k0 = pl.multiple_of(kt * SEL_KT, SEL_KT)
            k = ks_ref[0, pl.ds(k0, SEL_KT), :]
            v = vs_ref[0, pl.ds(k0, SEL_KT), :]
            s = lax.dot_general(qb, k, (((1,), (1,)), ((), ())), preferred_element_type=jnp.float32)
            mask, biases = tile_inputs(g, kt * n_blk, n_blk, False)
            maskf = mask.astype(jnp.float32)
            ps, alphas = [], []
            for r in range(NSA_REP):
                sl = slice(r * QT, (r + 1) * QT)
                sr = jnp.where(mask, s[sl] * scale + biases[r], NEG)
                m_old = m_ref[sl]
                m_new = jnp.maximum(m_old, jnp.max(sr, axis=-1, keepdims=True))
                p = jnp.exp(sr - m_new) * maskf
                alpha = jnp.exp(m_old - m_new)
                l_ref[sl] = alpha * l_ref[sl] + jnp.sum(p, axis=-1, keepdims=True)
                m_ref[sl] = m_new
                ps.append(p.astype(jnp.bfloat16))
                alphas.append(alpha)
            pv = jnp.dot(jnp.concatenate(ps, axis=0), v, preferred_element_type=jnp.float32)
            acc_ref[...] = jnp.concatenate(alphas, axis=0) * acc_ref[...] + pv
            return carry

        lax.fori_loop(0, qi // n_blk + 1, body, 0)
        return acc_ref[...] / jnp.maximum(l_ref[...], 1e-30)

    def window(g, qb):
        n_blk = WINDOW // QT + 1
        kb0 = jnp.maximum(qi - WINDOW // QT, 0)
        k0 = pl.multiple_of(kb0 * QT, QT)
        k = kw_ref[0, pl.ds(k0, n_blk * QT), :]
        v = vw_ref[0, pl.ds(k0, n_blk * QT), :]
        s = lax.dot_general(qb, k, (((1,), (1,)), ((), ())), preferred_element_type=jnp.float32)
        mask, biases = tile_inputs(g, kb0, n_blk, True)
        maskf = mask.astype(jnp.float32)
        ps, ls = [], []
        for r in range(NSA_REP):
            sr = jnp.where(mask, s[r * QT:(r + 1) * QT] * scale + biases[r], NEG)
            p = jnp.exp(sr - jnp.max(sr, axis=-1, keepdims=True)) * maskf
            ls.append(jnp.sum(p, axis=-1, keepdims=True))
            ps.append(p.astype(jnp.bfloat16))
        pv = jnp.dot(jnp.concatenate(ps, axis=0), v, preferred_element_type=jnp.float32)
        return pv / jnp.maximum(jnp.concatenate(ls, axis=0), 1e-30)

    outs = []
    for g in range(NSA_GROUPS):
        qf = q_ref[0, g].reshape(rows, LANES)
        qb = qf.astype(jnp.bfloat16)
        s = lax.dot_general(qf, kc_ref[0], (((1,), (1,)), ((), ())),
                            precision=lax.Precision.HIGHEST, preferred_element_type=jnp.float32)
        s = s * scale + bcmp_ref[g].reshape(rows, nb)
        cmask = jnp.concatenate([tq >= nidx * NSA_BLOCK + NSA_BLOCK - 1] * NSA_REP, axis=0)
        cmaskf = cmask.astype(jnp.float32)
        s = jnp.where(cmask, s, NEG)
        p = jnp.exp(s - jnp.max(s, axis=-1, keepdims=True)) * cmaskf
        pr = p / jnp.maximum(jnp.sum(p, axis=-1, keepdims=True), 1e-30)
        o_cmp = jnp.dot(pr.astype(jnp.bfloat16), vc_ref[0].astype(jnp.bfloat16), preferred_element_type=jnp.float32)
        imp = pr[0:QT]
        for r in range(1, NSA_REP):
            imp = imp + pr[r * QT:(r + 1) * QT]
        cur = tq // NSA_BLOCK
        forced = (nidx == cur) | (nidx == 0)
        score = jnp.where(nidx <= cur, jnp.where(forced, -NEG, imp), NEG)
        cnt = jnp.zeros((QT, nb), jnp.float32)
        for j in range(nb):
            col = score[:, j:j + 1]
            beats = (col > score) | ((col == score) & (nidx > j))
            cnt = cnt + beats.astype(jnp.float32)
        msel = (cnt < float(min(NSA_TOPN, nb))).astype(jnp.bfloat16)
        mexp = jnp.dot(msel, expand, preferred_element_type=jnp.float32)
        for kk in range(n_kt):
            mexp_ref[g, kk] = mexp[:, kk * QT:(kk + 1) * QT]
        o_sel = flash_sel(g, qb)
        o_win = window(g, qb)
        outs.append((o_cmp, o_sel, o_win))

    for r in range(NSA_REP):
        sl = slice(r * QT, (r + 1) * QT)
        ocmp_ref[0, r] = jnp.where(lane_lo, outs[0][0][sl], outs[1][0][sl])
        osel_ref[0, r] = jnp.where(lane_lo, outs[0][1][sl], outs[1][1][sl])
        owin_ref[0, r] = jnp.where(lane_lo, outs[0][2][sl], outs[1][2][sl])


def _bucket_thresholds():
    b = _rel_bucket(jnp.arange(MAX_DISTANCE))
    return jnp.sum(b[None, :] < jnp.arange(1, N_BUCKETS)[:, None], axis=1)


def _bias_of_dist(bt, th, dist):
    ex = (slice(None), slice(None)) + (None,) * dist.ndim
    d = jnp.maximum(dist, 0)[None, None]
    val = jnp.broadcast_to(bt[0][ex], (NSA_GROUPS, NSA_REP) + dist.shape)
    for kk in range(1, N_BUCKETS):
        val = jnp.where(d >= th[kk - 1], bt[kk][ex], val)
    return val


def _nsa_prompt_tables(rel_bias, T):
    nb = T // NSA_BLOCK
    bt = rel_bias.astype(jnp.float32).reshape(N_BUCKETS, NSA_GROUPS, NSA_REP)
    th = _bucket_thresholds()
    d = (jnp.arange(N_DELTA)[:, None, None] * QT + jnp.arange(QT)[None, :, None] - jnp.arange(QT)[None, None, :])
    tiles = _bias_of_dist(bt, th, d)
    dc = jnp.arange(T)[:, None] - (jnp.arange(nb)[None, :] * NSA_BLOCK + NSA_BLOCK - 1)
    bcmp = _bias_of_dist(bt, th, dc)
    return tiles, bcmp


def _group_padded_queries(qt):
    z = jnp.zeros_like(qt[:, 0])
    return jnp.stack([jnp.concatenate([qt[:, 0], z], axis=-1), jnp.concatenate([z, qt[:, 1]], axis=-1)], axis=1)


def _nsa_prompt(q_n, k_cmp, v_cmp, k_sel, v_sel, k_win, v_win, kc_gain, tables):
    B, T = q_n.shape[:2]
    assert T % SEL_KT == 0 and SEL_KT % QT == 0 and QT == 2 * NSA_BLOCK and T >= WINDOW + QT
    nb = T // NSA_BLOCK
    tiles, bcmp = tables
    kc, vc = _block_summaries(k_cmp, v_cmp, kc_gain)
    qt = q_n.transpose(0, 2, 3, 1, 4).reshape(B, NSA_GROUPS, NSA_REP * T, NSA_DIM)
    q_pad = _group_padded_queries(qt).reshape(B, NSA_GROUPS, NSA_REP, T, LANES)
    bf = lambda a: a.astype(jnp.bfloat16)
    kv_spec = pl.BlockSpec((1, T, LANES), lambda b, i: (b, 0, 0))
    sm_spec = pl.BlockSpec((1, nb, LANES), lambda b, i: (b, 0, 0))
    o_spec = pl.BlockSpec((1, NSA_REP, QT, LANES), lambda b, i: (b, 0, i, 0))
    o_shape = jax.ShapeDtypeStruct((B, NSA_REP, T, LANES), jnp.float32)
    outs = pl.pallas_call(
        _nsa_prompt_kernel,
        grid=(B, T // QT),
        in_specs=[pl.BlockSpec((1, NSA_GROUPS, NSA_REP, QT, LANES), lambda b, i: (b, 0, 0, i, 0)),
                  sm_spec, sm_spec,
                  pl.BlockSpec((NSA_GROUPS, NSA_REP, QT, nb), lambda b, i: (0, 0, i, 0)),
                  kv_spec, kv_spec, kv_spec, kv_spec,
                  pl.BlockSpec((NSA_GROUPS, NSA_REP, N_DELTA, QT, QT), lambda b, i: (0, 0, 0, 0, 0))],
        out_specs=[o_spec, o_spec, o_spec],
        out_shape=[o_shape, o_shape, o_shape],
        scratch_shapes=[pltpu.VMEM((NSA_GROUPS, T // QT, QT, QT), jnp.float32),
                        pltpu.VMEM((NSA_REP * QT, 1), jnp.float32),
                        pltpu.VMEM((NSA_REP * QT, 1), jnp.float32),
                        pltpu.VMEM((NSA_REP * QT, LANES), jnp.float32)],
        compiler_params=pltpu.CompilerParams(
            dimension_semantics=("parallel", "arbitrary"), vmem_limit_bytes=VMEM_LIMIT_BYTES),
        name="nsa_prompt_attention",
    )(q_pad, kc, vc, bcmp, bf(k_sel), bf(v_sel), bf(k_win), bf(v_win), tiles)
    back = lambda o: o.reshape(B, NSA_REP, T, NSA_GROUPS, NSA_DIM).transpose(0, 2, 3, 1, 4)
    return tuple(back(o) for o in outs)


def _mla_sample_kernel(layer, n_pages, pt_ref, cache_ref, new_ref, qn_ref, qpe_ref, gn_ref, wuk_ref, wuv_ref,
                       tbl_ref, o_ref, buf_ref, sem_ref, m_ref, l_ref, acc_ref):
    b = pl.program_id(0)
    n_seq = pl.num_programs(0)
    cp = MLA_CHUNK_PAGES
    lc = cp * PAGE_SIZE
    n_chunks = n_pages // cp
    scale = MLA_QK ** -0.5

    def page_copy(seq, chunk, p, slot):
        page = pt_ref[seq, chunk * cp + p]
        return pltpu.make_async_copy(cache_ref.at[layer, page], buf_ref.at[slot, :, pl.ds(p * PAGE_SIZE, PAGE_SIZE)],
                                     sem_ref.at[slot])

    def start_chunk(seq, chunk, slot):
        for p in range(cp):
            page_copy(seq, chunk, p, slot).start()

    def wait_chunk(seq, chunk, slot):
        for p in range(cp):
            page_copy(seq, chunk, p, slot).wait()

    @pl.when(b == 0)
    def _():
        start_chunk(0, 0, 0)

    m_ref[...] = jnp.full(m_ref.shape, NEG, jnp.float32)
    l_ref[...] = jnp.zeros(l_ref.shape, jnp.float32)
    acc_ref[...] = jnp.zeros(acc_ref.shape, jnp.float32)

    qn = (qn_ref[0] * gn_ref[...]).astype(jnp.bfloat16)
    a_abs = jnp.dot(qn, wuk_ref[...], preferred_element_type=jnp.float32)
    lhs1 = jnp.concatenate([wuk_ref[...], a_abs.astype(jnp.bfloat16)], axis=0)
    qpe = qpe_ref[0].astype(jnp.bfloat16)
    n_kn = MLA_HEADS * MLA_NOPE

    def attend(rows_t, col0, mask):
        L = rows_t.shape[1]
        lat = rows_t[:KV_LORA].astype(jnp.bfloat16)
        kpe = rows_t[KV_LORA:]
        x = jnp.dot(lhs1, lat, preferred_element_type=jnp.float32)
        ss_pe = jnp.sum(kpe * kpe, axis=0, keepdims=True)
        f = (jnp.concatenate([kpe, kpe], axis=0) * tbl_ref[:, pl.ds(col0, L)]).astype(jnp.bfloat16)
        s_pe = jnp.dot(qpe, f, preferred_element_type=jnp.float32)
        ps, alphas = [], []
        for h in range(MLA_HEADS):
            kn = x[h * MLA_NOPE:(h + 1) * MLA_NOPE]
            ss = jnp.sum(kn * kn, axis=0, keepdims=True) + ss_pe
            r = lax.rsqrt(ss * (1.0 / MLA_QK) + EPS) * scale
            sl = slice(h * DEC_SEQ, (h + 1) * DEC_SEQ)
            s = (x[n_kn + h * DEC_SEQ:n_kn + (h + 1) * DEC_SEQ] + s_pe[sl]) * r
            if mask is not None:
                s = jnp.where(mask, s, NEG)
            m_old = m_ref[sl]
            m_new = jnp.maximum(m_old, jnp.max(s, axis=-1, keepdims=True))
            p = jnp.exp(s - m_new)
            if mask is not None:
                p = p * mask.astype(jnp.float32)
            alpha = jnp.exp(m_old - m_new)
            l_ref[sl] = alpha * l_ref[sl] + jnp.sum(p, axis=-1, keepdims=True)
            m_ref[sl] = m_new
            ps.append(p.astype(jnp.bfloat16))
            alphas.append(alpha)
        pv = lax.dot_general(jnp.concatenate(ps, axis=0), lat, (((1,), (1,)), ((), ())),
                             preferred_element_type=jnp.float32)
        acc_ref[...] = jnp.concatenate(alphas, axis=0) * acc_ref[...] + pv

    def chunk_body(c, carry):
        slot = c % 2

        @pl.when(c + 1 < n_chunks)
        def _():
            start_chunk(b, c + 1, 1 - slot)

        @pl.when((c + 1 == n_chunks) & (b + 1 < n_seq))
        def _():
            start_chunk(b + 1, 0, 1 - slot)

        wait_chunk(b, c, slot)
        attend(buf_ref[slot], pl.multiple_of(c * lc, lc), None)
        return carry

    lax.fori_loop(0, n_chunks, chunk_body, 0)

    tt = lax.broadcasted_iota(jnp.int32, (DEC_SEQ, PAGE_SIZE), 0)
    jj = lax.broadcasted_iota(jnp.int32, (DEC_SEQ, PAGE_SIZE), 1)
    attend(new_ref[0], n_pages * PAGE_SIZE, jj <= tt)

    o_lat = (acc_ref[...] / jnp.maximum(l_ref[...], 1e-30)).astype(jnp.bfloat16)
    for h in range(MLA_HEADS):
        o_ref[0, h * DEC_SEQ:(h + 1) * DEC_SEQ, :] = jnp.dot(o_lat[h * DEC_SEQ:(h + 1) * DEC_SEQ], wuv_ref[h],
                                                             preferred_element_type=jnp.float32)


def _mla_sample(q_m, rows_new, page_table, cache_mla_t, layer, w_ukv, k_gain, cos, sin):
    DB, S = q_m.shape[:2]
    n_pages = page_table.shape[1]
    assert S == DEC_SEQ and n_pages % MLA_CHUNK_PAGES == 0 and (n_pages // MLA_CHUNK_PAGES) % 2 == 0
    n_pos = n_pages * PAGE_SIZE + PAGE_SIZE
    f32 = jnp.float32
    half = MLA_ROPE // 2
    w = w_ukv.reshape(KV_LORA, MLA_HEADS, MLA_NOPE + MLA_V)
    w_uk = w[:, :, :MLA_NOPE].reshape(KV_LORA, MLA_HEADS * MLA_NOPE).T.astype(jnp.bfloat16)
    w_uv = w[:, :, MLA_NOPE:].transpose(1, 0, 2).astype(jnp.bfloat16)
    qt = q_m.transpose(0, 2, 1, 3)
    eye_h = jnp.eye(MLA_HEADS, dtype=f32)
    q_blk = (qt[:, :, :, None, :MLA_NOPE] * eye_h[None, :, None, :, None]).reshape(
        DB, MLA_ROWS_Q, MLA_HEADS * MLA_NOPE)
    q1, q2 = qt[..., MLA_NOPE:MLA_NOPE + half], qt[..., MLA_NOPE + half:]
    q_pe = jnp.concatenate([q1, q2, q2, -q1], axis=-1).reshape(DB, MLA_ROWS_Q, 2 * MLA_ROPE)
    gn = jnp.tile(k_gain[:MLA_NOPE].astype(f32), MLA_HEADS).reshape(1, MLA_HEADS * MLA_NOPE)
    g1, g2 = k_gain[MLA_NOPE:MLA_NOPE + half].astype(f32), k_gain[MLA_NOPE + half:].astype(f32)
    c, s = cos[:n_pos].T, sin[:n_pos].T
    tbl = jnp.concatenate([c * g1[:, None], c * g2[:, None], s * g1[:, None], s * g2[:, None]], axis=0)
    new_pad = jnp.pad(rows_new.transpose(0, 2, 1), ((0, 0), (0, 0), (0, PAGE_SIZE - S)))
    lc = MLA_CHUNK_PAGES * PAGE_SIZE
    const = lambda shape: pl.BlockSpec(shape, lambda b, pt: (0,) * len(shape))
    out = pl.pallas_call(
        functools.partial(_mla_sample_kernel, layer, n_pages),
        grid_spec=pltpu.PrefetchScalarGridSpec(
            num_scalar_prefetch=1,
            grid=(DB,),
            in_specs=[pl.BlockSpec(memory_space=pl.ANY),
                      pl.BlockSpec((1, MLA_CACHE, PAGE_SIZE), lambda b, pt: (b, 0, 0)),
                      pl.BlockSpec((1, MLA_ROWS_Q, MLA_HEADS * MLA_NOPE), lambda b, pt: (b, 0, 0)),
                      pl.BlockSpec((1, MLA_ROWS_Q, 2 * MLA_ROPE), lambda b, pt: (b, 0, 0)),
                      const((1, MLA_HEADS * MLA_NOPE)),
                      const((MLA_HEADS * MLA_NOPE, KV_LORA)),
                      const((MLA_HEADS, KV_LORA, MLA_V)),
                      const((2 * MLA_ROPE, n_pos))],
            out_specs=pl.BlockSpec((1, MLA_ROWS_Q, MLA_V), lambda b, pt: (b, 0, 0)),
            scratch_shapes=[pltpu.VMEM((2, MLA_CACHE, lc), f32),
                            pltpu.SemaphoreType.DMA((2,)),
                            pltpu.VMEM((MLA_ROWS_Q, 1), f32),
                            pltpu.VMEM((MLA_ROWS_Q, 1), f32),
                            pltpu.VMEM((MLA_ROWS_Q, KV_LORA), f32)]),
        out_shape=jax.ShapeDtypeStruct((DB, MLA_ROWS_Q, MLA_V), f32),
        compiler_params=pltpu.CompilerParams(dimension_semantics=("arbitrary",), vmem_limit_bytes=VMEM_LIMIT_BYTES),
        name="mla_sample_attention",
    )(page_table, cache_mla_t, new_pad, q_blk, q_pe, gn, w_uk, w_uv, tbl)
    return out.reshape(DB, MLA_HEADS, S, MLA_V).transpose(0, 2, 1, 3)


def _nsa_sample_kernel(layer, n_pages, pt_ref, ccmp_ref, csel_ref, q_ref, ncmp_ref, nsel_ref, nwin_ref, wbuf_ref,
                       gain_ref, bcmp_ref, bfar_ref, blast_ref, bnew_ref, bwin_ref, expand_ref, pool_ref,
                       ocmp_ref, osel_ref, owin_ref, buf_ref, sem_ref, sumc_ref, sum_ref, msel_ref,
                       m_ref, l_ref, acc_ref):
    b = pl.program_id(0)
    n_seq = pl.num_programs(0)
    cp = NSA_CHUNK_PAGES
    lc = cp * PAGE_SIZE
    n_chunks = n_pages // cp
    bpc = lc // NSA_BLOCK
    n_past = n_pages * PAGE_SIZE
    nbp = n_past // NSA_BLOCK
    nbs = sum_ref.shape[1]
    scale = NSA_DIM ** -0.5

    def page_copy(cache_ref, seq, chunk, p, slot):
        page = pt_ref[seq, chunk * cp + p]
        return pltpu.make_async_copy(cache_ref.at[layer, page], buf_ref.at[slot, :, pl.ds(p * PAGE_SIZE, PAGE_SIZE)],
                                     sem_ref.at[slot])

    def start_chunk(cache_ref, seq, chunk, slot):
        for p in range(cp):
            page_copy(cache_ref, seq, chunk, p, slot).start()

    def wait_chunk(cache_ref, seq, chunk, slot):
        for p in range(cp):
            page_copy(cache_ref, seq, chunk, p, slot).wait()

    @pl.when(b == 0)
    def _():
        start_chunk(ccmp_ref, 0, 0, 0)

    def cmp_body(c, carry):
        slot = c % 2

        @pl.when(c + 1 < n_chunks)
        def _():
            start_chunk(ccmp_ref, b, c + 1, 1 - slot)

        @pl.when(c + 1 == n_chunks)
        def _():
            start_chunk(csel_ref, b, 0, 1 - slot)

        wait_chunk(ccmp_ref, b, c, slot)
        x = buf_ref[slot]
        hi = x.astype(jnp.bfloat16)
        lo = (x - hi.astype(jnp.float32)).astype(jnp.bfloat16)
        sumc_ref[c] = (jnp.dot(hi, pool_ref[...], preferred_element_type=jnp.float32)
                       + jnp.dot(lo, pool_ref[...], preferred_element_type=jnp.float32))
        return carry

    lax.fori_loop(0, n_chunks, cmp_body, 0)
    sum_ref[...] = jnp.zeros(sum_ref.shape, jnp.float32)
    for c in range(n_chunks):
        sum_ref[:, c * bpc:(c + 1) * bpc] = sumc_ref[c][:, 0:bpc]
    new_mean = jnp.sum(ncmp_ref[0], axis=1, keepdims=True) * (1.0 / NSA_BLOCK)
    col = lax.broadcasted_iota(jnp.int32, (KV_LANES, nbs), 1)
    summ = jnp.where(col == nbp, new_mean, sum_ref[...])

    tt = lax.broadcasted_iota(jnp.int32, (ROWS_Q, nbs), 0) % DEC_SEQ
    nn = lax.broadcasted_iota(jnp.int32, (ROWS_Q, nbs), 1)
    cmask = (n_past + tt) >= (nn * NSA_BLOCK + NSA_BLOCK - 1)
    cmaskf = cmask.astype(jnp.float32)
    o_cmp, imps = [], []
    for g in range(NSA_GROUPS):
        km = summ[g * NSA_DIM:(g + 1) * NSA_DIM]
        vm = summ[NSA_KV + g * NSA_DIM:NSA_KV + (g + 1) * NSA_DIM]
        ss = jnp.sum(km * km, axis=0, keepdims=True)
        kc = km * lax.rsqrt(ss * (1.0 / NSA_DIM) + EPS) * gain_ref[...]
        s = jnp.dot(q_ref[0, g], kc, precision=lax.Precision.HIGHEST, preferred_element_type=jnp.float32)
        s = jnp.where(cmask, s * scale + bcmp_ref[g], NEG)
        p = jnp.exp(s - jnp.max(s, axis=-1, keepdims=True)) * cmaskf
        pr = p / jnp.maximum(jnp.sum(p, axis=-1, keepdims=True), 1e-30)
        o_cmp.append(lax.dot_general(pr.astype(jnp.bfloat16), vm.astype(jnp.bfloat16), (((1,), (1,)), ((), ())),
                                     preferred_element_type=jnp.float32))
        imp = pr[0:DEC_SEQ]
        for r in range(1, NSA_REP):
            imp = imp + pr[r * DEC_SEQ:(r + 1) * DEC_SEQ]
        imps.append(imp)
    imp = jnp.concatenate(imps, axis=0)
    n16 = lax.broadcasted_iota(jnp.int32, imp.shape, 1)
    t16 = lax.broadcasted_iota(jnp.int32, imp.shape, 0) % DEC_SEQ
    cur = (n_past + t16) // NSA_BLOCK
    forced = (n16 == cur) | (n16 == 0)
    score = jnp.where(n16 <= cur, jnp.where(forced, -NEG, imp), NEG)
    cnt = jnp.zeros(imp.shape, jnp.float32)
    for j in range(nbp + 1):
        cj = score[:, j:j + 1]
        beats = (cj > score) | ((cj == score) & (n16 > j))
        cnt = cnt + beats.astype(jnp.float32)
    msel = (cnt < float(NSA_TOPN)).astype(jnp.float32)
    msel_ref[...] = jnp.zeros(msel_ref.shape, jnp.float32)
    for c in range(n_chunks):
        msel_ref[c, :, 0:bpc] = msel[:, c * bpc:(c + 1) * bpc]
    msel_new = msel[:, nbp:nbp + 1]

    def attend(g, kv_t, bias, mask):
        qb = q_ref[0, g].astype(jnp.bfloat16)
        k = kv_t[g * NSA_DIM:(g + 1) * NSA_DIM].astype(jnp.bfloat16)
        v = kv_t[NSA_KV + g * NSA_DIM:NSA_KV + (g + 1) * NSA_DIM].astype(jnp.bfloat16)
        s = jnp.dot(qb, k, preferred_element_type=jnp.float32)
        s = jnp.where(mask, s * scale + bias, NEG)
        m_old = m_ref[g]
        m_new = jnp.maximum(m_old, jnp.max(s, axis=-1, keepdims=True))
        p = jnp.exp(s - m_new) * mask.astype(jnp.float32)
        alpha = jnp.exp(m_old - m_new)
        l_ref[g] = alpha * l_ref[g] + jnp.sum(p, axis=-1, keepdims=True)
        m_ref[g] = m_new
        acc_ref[g] = alpha * acc_ref[g] + lax.dot_general(p.astype(jnp.bfloat16), v, (((1,), (1,)), ((), ())),
                                                          preferred_element_type=jnp.float32)

    def reset():
        m_ref[...] = jnp.full(m_ref.shape, NEG, jnp.float32)
        l_ref[...] = jnp.zeros(l_ref.shape, jnp.float32)
        acc_ref[...] = jnp.zeros(acc_ref.shape, jnp.float32)

    def result(g):
        return acc_ref[g] / jnp.maximum(l_ref[g], 1e-30)

    tile6 = lambda a: jnp.concatenate([a] * NSA_REP, axis=0)
    tn = lax.broadcasted_iota(jnp.int32, (ROWS_Q, PAGE_SIZE), 0) % DEC_SEQ
    jn = lax.broadcasted_iota(jnp.int32, (ROWS_Q, PAGE_SIZE), 1)
    new_causal = jn <= tn

    reset()

    def sel_body(c, carry):
        slot = c % 2

        @pl.when(c + 1 < n_chunks)
        def _():
            start_chunk(csel_ref, b, c + 1, 1 - slot)

        @pl.when((c + 1 == n_chunks) & (b + 1 < n_seq))
        def _():
            start_chunk(ccmp_ref, b + 1, 0, 1 - slot)

        wait_chunk(csel_ref, b, c, slot)
        kv_t = buf_ref[slot]
        mexp = jnp.dot(msel_ref[c].astype(jnp.bfloat16), expand_ref[...], preferred_element_type=jnp.float32)
        last = jnp.where(c == n_chunks - 1, 1.0, 0.0).astype(jnp.float32)
        for g in range(NSA_GROUPS):
            bias = bfar_ref[g] + last * blast_ref[g]
            attend(g, kv_t, bias, tile6(mexp[g * DEC_SEQ:(g + 1) * DEC_SEQ]) > 0.5)
        return carry

    lax.fori_loop(0, n_chunks, sel_body, 0)
    for g in range(NSA_GROUPS):
        attend(g, nsel_ref[0], bnew_ref[g], new_causal & (tile6(msel_new[g * DEC_SEQ:(g + 1) * DEC_SEQ]) > 0.5))
        osel_ref[0, g] = result(g)

    reset()
    n_buf = wbuf_ref.shape[3]
    tw = lax.broadcasted_iota(jnp.int32, (ROWS_Q, n_buf), 0) % DEC_SEQ
    cw = lax.broadcasted_iota(jnp.int32, (ROWS_Q, n_buf), 1)
    dist_w = n_buf + tw - cw
    wmask = (dist_w >= 0) & (dist_w < WINDOW)
    for g in range(NSA_GROUPS):
        attend(g, wbuf_ref[0, 0], bwin_ref[g], wmask)
        attend(g, nwin_ref[0], bnew_ref[g], new_causal)
        owin_ref[0, g] = result(g)
        ocmp_ref[0, g] = o_cmp[g]


def _nsa_sample_tables(rel_bias, n_pages, n_buf):
    n_past = n_pages * PAGE_SIZE
    nbp = n_past // NSA_BLOCK
    nbs = -(-(nbp + 1) // LANES) * LANES
    lc = NSA_CHUNK_PAGES * PAGE_SIZE
    bt = rel_bias.astype(jnp.float32).reshape(N_BUCKETS, NSA_GROUPS, NSA_REP)
    th = _bucket_thresholds()
    t = jnp.arange(DEC_SEQ)

    def table(dist):
        return _bias_of_dist(bt, th, dist).reshape(NSA_GROUPS, ROWS_Q, dist.shape[1])

    bcmp = table(n_past + t[:, None] - (jnp.arange(nbs)[None, :] * NSA_BLOCK + NSA_BLOCK - 1))
    bfar = jnp.broadcast_to(bt[N_BUCKETS - 1][:, :, None, None], (NSA_GROUPS, NSA_REP, DEC_SEQ, 1)).reshape(
        NSA_GROUPS, ROWS_Q, 1)
    blast = table(n_past + t[:, None] - (n_past - lc + jnp.arange(lc))[None, :]) - bfar
    bnew = table(t[:, None] - jnp.arange(PAGE_SIZE)[None, :])
    bwin = table(n_buf + t[:, None] - jnp.arange(n_buf)[None, :])
    expand = (jnp.arange(lc)[None, :] // NSA_BLOCK == jnp.arange(LANES)[:, None]).astype(jnp.bfloat16)
    pool = (expand.T.astype(jnp.float32) * (1.0 / NSA_BLOCK)).astype(jnp.bfloat16)
    return bcmp, bfar, blast, bnew, bwin, expand, pool


def _rows_last(a):
    nd = a.ndim
    perm = tuple(range(nd - 4)) + (nd - 3, nd - 2, nd - 1, nd - 4)
    at = a.transpose(perm)
    return at.reshape(at.shape[:-4] + (KV_LANES, at.shape[-1]))


def _nsa_sample(q_n, kc_new, vc_new, ks_new, vs_new, kw_new, vw_new, page_table, cache_cmp_t, cache_sel_t, win_t,
                layer, kc_gain, tables):
    DB, S = q_n.shape[:2]
    n_pages = page_table.shape[1]
    n_buf = win_t.shape[3]
    lc = NSA_CHUNK_PAGES * PAGE_SIZE
    n_chunks = n_pages // NSA_CHUNK_PAGES
    assert S == DEC_SEQ and n_pages % NSA_CHUNK_PAGES == 0 and n_chunks % 2 == 0 and n_buf == WINDOW
    assert lc + 1 >= MAX_DISTANCE and lc // NSA_BLOCK <= LANES
    f32 = jnp.float32
    bcmp, bfar, blast, bnew, bwin, expand, pool = tables
    nbs = bcmp.shape[-1]
    qt = q_n.transpose(0, 2, 3, 1, 4).reshape(DB, NSA_GROUPS, ROWS_Q, NSA_DIM)
    new_t = lambda k, v, width: jnp.pad(jnp.concatenate([k, v], axis=-1).transpose(0, 2, 1),
                                        ((0, 0), (0, 0), (0, width - S)))
    new_cmp = new_t(kc_new, vc_new, LANES)
    new_sel = new_t(ks_new, vs_new, PAGE_SIZE)
    new_win = new_t(kw_new, vw_new, PAGE_SIZE)
    gain = kc_gain.astype(f32).reshape(NSA_DIM, 1)
    const = lambda shape: pl.BlockSpec(shape, lambda b, pt: (0,) * len(shape))
    per_seq = lambda shape: pl.BlockSpec((1,) + shape, lambda b, pt: (b,) + (0,) * len(shape))
    o_shape = jax.ShapeDtypeStruct((DB, NSA_GROUPS, ROWS_Q, NSA_DIM), f32)
    outs = pl.pallas_call(
        functools.partial(_nsa_sample_kernel, layer, n_pages),
        grid_spec=pltpu.PrefetchScalarGridSpec(
            num_scalar_prefetch=1,
            grid=(DB,),
            in_specs=[pl.BlockSpec(memory_space=pl.ANY), pl.BlockSpec(memory_space=pl.ANY),
                      per_seq((NSA_GROUPS, ROWS_Q, NSA_DIM)),
                      per_seq((KV_LANES, LANES)), per_seq((KV_LANES, PAGE_SIZE)), per_seq((KV_LANES, PAGE_SIZE)),
                      pl.BlockSpec((1, 1, KV_LANES, n_buf), lambda b, pt: (layer, b, 0, 0)),
                      const((NSA_DIM, 1)),
                      const((NSA_GROUPS, ROWS_Q, nbs)), const((NSA_GROUPS, ROWS_Q, 1)),
                      const((NSA_GROUPS, ROWS_Q, lc)), const((NSA_GROUPS, ROWS_Q, PAGE_SIZE)),
                      const((NSA_GROUPS, ROWS_Q, n_buf)), const((LANES, lc)), const((lc, LANES))],
            out_specs=[per_seq((NSA_GROUPS, ROWS_Q, NSA_DIM))] * 3,
            scratch_shapes=[pltpu.VMEM((2, KV_LANES, lc), f32),
                            pltpu.SemaphoreType.DMA((2,)),
                            pltpu.VMEM((n_chunks, KV_LANES, LANES), f32),
                            pltpu.VMEM((KV_LANES, nbs), f32),
                            pltpu.VMEM((n_chunks, NSA_GROUPS * DEC_SEQ, LANES), f32),
                            pltpu.VMEM((NSA_GROUPS, ROWS_Q, 1), f32),
                            pltpu.VMEM((NSA_GROUPS, ROWS_Q, 1), f32),
                            pltpu.VMEM((NSA_GROUPS, ROWS_Q, NSA_DIM), f32)]),
        out_shape=[o_shape] * 3,
        compiler_params=pltpu.CompilerParams(dimension_semantics=("arbitrary",), vmem_limit_bytes=VMEM_LIMIT_BYTES),
        name="nsa_sample_attention",
    )(page_table, cache_cmp_t, cache_sel_t, qt, new_cmp, new_sel, new_win, win_t, gain,
      bcmp, bfar, blast, bnew, bwin, expand, pool)
    back = lambda o: o.reshape(DB, NSA_GROUPS, NSA_REP, S, NSA_DIM).transpose(0, 3, 1, 2, 4)
    return tuple(back(o) for o in outs)


def _pool_kernel(pos0, halo_ref, u_ref, w_ref, scale_ref, o_ref, ext_ref):
    i = pl.program_id(1)
    tile = u_ref.shape[1]

    @pl.when(i == 0)
    def _():
        ext_ref[0:POOL_HALO] = halo_ref[0]

    @pl.when(i > 0)
    def _():
        ext_ref[0:POOL_HALO] = ext_ref[tile:tile + POOL_HALO]

    ext_ref[POOL_HALO:POOL_HALO + tile] = u_ref[0]
    pos = pos0 + i * tile + lax.broadcasted_iota(jnp.int32, (tile, 1), 0)
    for gi, w in enumerate(POOL_WINDOWS):
        cols = slice(gi * POOL_GROUP, (gi + 1) * POOL_GROUP)
        x = ext_ref[POOL_HALO:POOL_HALO + tile, cols]
        win_sum = x
        for k in range(1, w):
            win_sum = win_sum + ext_ref[POOL_HALO - k:POOL_HALO - k + tile, cols]
        cnt = jnp.minimum(w, pos + 1).astype(jnp.float32)
        d = win_sum / cnt - x
        y = jnp.dot(d.astype(jnp.bfloat16), w_ref[gi], preferred_element_type=jnp.float32)
        o_ref[0, :, cols] = y * scale_ref[:, cols]


def _pool_mix(u, halo, pos0, w_pool, scale, tile):
    N, L, _ = u.shape
    assert L % tile == 0 and tile % 8 == 0 and (tile >= POOL_HALO or L == tile)
    return pl.pallas_call(
        functools.partial(_pool_kernel, pos0),
        grid=(N, L // tile),
        in_specs=[pl.BlockSpec((1, POOL_HALO, POOL_WIDTH), lambda n, i: (n, 0, 0)),
                  pl.BlockSpec((1, tile, POOL_WIDTH), lambda n, i: (n, i, 0)),
                  pl.BlockSpec((len(POOL_WINDOWS), POOL_GROUP, POOL_GROUP), lambda n, i: (0, 0, 0)),
                  pl.BlockSpec((1, POOL_WIDTH), lambda n, i: (0, 0))],
        out_specs=pl.BlockSpec((1, tile, POOL_WIDTH), lambda n, i: (n, i, 0)),
        out_shape=jax.ShapeDtypeStruct((N, L, POOL_WIDTH), jnp.float32),
        scratch_shapes=[pltpu.VMEM((POOL_HALO + tile, POOL_WIDTH), jnp.float32)],
        compiler_params=pltpu.CompilerParams(
            dimension_semantics=("parallel", "arbitrary"), vmem_limit_bytes=VMEM_LIMIT_BYTES),
        name="pool_mix",
    )(halo, u, w_pool.astype(jnp.bfloat16), scale.astype(jnp.float32).reshape(1, POOL_WIDTH))


def _project(x, pos, p):
    z = _mm(_rmsnorm(x, p['attn_norm']), p['w_in'], tn=384)
    (u_pool, q_lat, kv_lat, k_pe, q_nsa, k_cmp, v_cmp, k_sel, v_sel, k_win, v_win, g_logit) = _split_in(z)
    N, L = x.shape[:2]
    grp = lambda a: a.reshape(N, L, NSA_GROUPS, NSA_DIM)
    q_mla = _mla_q(q_lat, p, pos)
    mla_rows = jnp.concatenate([_rmsnorm(kv_lat, p['mla_kv_norm']), k_pe], axis=-1)
    q_n = _rmsnorm(q_nsa.reshape(N, L, NSA_GROUPS, NSA_REP, NSA_DIM), p['nsa_q_gain'])
    k_sel = _rmsnorm(grp(k_sel), p['nsa_k_gain'][1])
    k_win = _rmsnorm(grp(k_win), p['nsa_k_gain'][2])
    gates = jax.nn.sigmoid(g_logit).reshape(N, L, 3, NSA_GROUPS, NSA_REP)
    return u_pool, q_mla, mla_rows, q_n, grp(k_cmp), grp(v_cmp), k_sel, grp(v_sel), k_win, grp(v_win), gates


def _finish(x, pool_o, mla_o, nsa_o, p):
    N, L = x.shape[:2]
    mix = jnp.concatenate([pool_o, mla_o.reshape(N, L, MLA_WIDTH), nsa_o.reshape(N, L, NSA_WIDTH)], axis=-1)
    x = x + _mm(mix, p['w_out'])
    h = _rmsnorm(x, p['ffn_norm'])
    a = jax.nn.silu(_mm(h, p['w_gate'])) * _mm(h, p['w_up'])
    return x + _mm(a, p['w_down'])


def kernel(x_prompt, x_sample, cache_mla, cache_nsa_cmp, cache_nsa_sel, state_nsa_win, state_pool, page_table,
           rel_bias, attn_norm, w_in, pool_w, pool_scale, mla_q_norm, mla_kv_norm, w_uq, w_ukv,
           mla_q_gain, mla_k_gain, nsa_q_gain, nsa_k_gain, w_out, ffn_norm, w_gate, w_up, w_down):
    xp, xs = x_prompt, x_sample
    B, T = xp.shape[:2]
    DB, S = xs.shape[:2]
    n_pages = page_table.shape[1]
    n_buf = state_nsa_win.shape[2]
    n_ctx = state_pool.shape[2]
    pos_p = jnp.arange(T)
    pos_s = PAST_LEN + jnp.arange(S)
    cos, sin = _rope_tables(n_pages * PAGE_SIZE + PAGE_SIZE)
    prompt_tables = _nsa_prompt_tables(rel_bias, T)
    sample_tables = _nsa_sample_tables(rel_bias, n_pages, n_buf)
    cache_mla_t = cache_mla.transpose(0, 1, 3, 2)
    cache_cmp_t = _rows_last(cache_nsa_cmp)
    cache_sel_t = _rows_last(cache_nsa_sel)
    win_state_t = _rows_last(state_nsa_win)
    pool_halo_p = jnp.zeros((B, POOL_HALO, POOL_WIDTH), jnp.float32)
    pool_halo_s = jnp.pad(state_pool, ((0, 0), (0, 0), (POOL_HALO - n_ctx, 0), (0, 0)))
    flat = lambda a: a.reshape(a.shape[:2] + (NSA_KV,))
    nmla_p, ncmp_p, nsel_p, nwin_p, npool_p = [], [], [], [], []
    nmla_s, ncmp_s, nsel_s, nwin_s, npool_s = [], [], [], [], []
    for l in range(DEPTH):
        p = {'attn_norm': attn_norm[l], 'w_in': w_in[l], 'mla_q_norm': mla_q_norm[l],
             'mla_kv_norm': mla_kv_norm[l], 'w_uq': w_uq[l], 'w_ukv': w_ukv[l],
             'mla_q_gain': mla_q_gain[l], 'mla_k_gain': mla_k_gain[l], 'nsa_q_gain': nsa_q_gain[l],
             'nsa_k_gain': nsa_k_gain[l], 'w_out': w_out[l], 'ffn_norm': ffn_norm[l],
             'w_gate': w_gate[l], 'w_up': w_up[l], 'w_down': w_down[l]}

        u, q_m, rows_m, q_n, kc, vc, ks, vs, kw, vw, gates = _project(xp, pos_p, p)
        pool_o = _pool_mix(u, pool_halo_p, 0, pool_w[l], pool_scale[l], POOL_TILE)
        k_m, v_m = _mla_kv(rows_m, p, pos_p)
        mla_o = jax.vmap(_mla_prompt_seq)(q_m, k_m, v_m)
        o_cmp, o_sel, o_win = _nsa_prompt(q_n, flat(kc), flat(vc), flat(ks), flat(vs), flat(kw), flat(vw),
                                          p['nsa_k_gain'][0], prompt_tables)
        nsa_o = _nsa_combine(gates, o_cmp, o_sel, o_win)
        xp = _finish(xp, pool_o, mla_o, nsa_o, p)
        nmla_p.append(rows_m)
        ncmp_p.append(jnp.stack([kc, vc], axis=2))
        nsel_p.append(jnp.stack([ks, vs], axis=2))
        nwin_p.append(jnp.stack([kw, vw], axis=2)[:, T - min(WINDOW, T):])
        npool_p.append(u[:, T - POOL_STATE:])

        u, q_m, rows_m, q_n, kc, vc, ks, vs, kw, vw, gates = _project(xs, pos_s, p)
        u_ext = jnp.concatenate([state_pool[l], u], axis=1)
        pool_o = _pool_mix(u, pool_halo_s[l], PAST_LEN, pool_w[l], pool_scale[l], S)
        mla_o = _mla_sample(q_m, rows_m, page_table, cache_mla_t, l, p['w_ukv'], p['mla_k_gain'], cos, sin)
        o_cmp, o_sel, o_win = _nsa_sample(q_n, flat(kc), flat(vc), flat(ks), flat(vs), flat(kw), flat(vw),
                                          page_table, cache_cmp_t, cache_sel_t, win_state_t, l,
                                          p['nsa_k_gain'][0], sample_tables)
        nsa_o = _nsa_combine(gates, o_cmp, o_sel, o_win)
        xs = _finish(xs, pool_o, mla_o, nsa_o, p)
        nmla_s.append(rows_m)
        ncmp_s.append(jnp.stack([kc, vc], axis=2))
        nsel_s.append(jnp.stack([ks, vs], axis=2))
        nwin_s.append(jnp.concatenate([state_nsa_win[l], jnp.stack([kw, vw], axis=2)], axis=1)[:, S:S + n_buf])
        npool_s.append(u_ext[:, S:S + n_ctx])

    return (xp, xs,
            jnp.stack(nmla_p), jnp.stack(ncmp_p), jnp.stack(nsel_p), jnp.stack(nwin_p), jnp.stack(npool_p),
            jnp.stack(nmla_s), jnp.stack(ncmp_s), jnp.stack(nsel_s), jnp.stack(nwin_s), jnp.stack(npool_s))
```

```python
import functools
import math

import jax
import jax.numpy as jnp
import numpy as np
from jax import lax
from jax.experimental import pallas as pl
from jax.experimental.pallas import tpu as pltpu

D_MODEL = 2048
BATCH = 2
SEQ = 4096
DEPTH = 2
DEC_BATCH = 128
DEC_SEQ = 8
PAST_LEN = 16384
PAGE_SIZE = 128
POOL_WINDOWS = (2, 4, 8, 16)
POOL_WIDTH = D_MODEL // 4
POOL_GROUP = POOL_WIDTH // len(POOL_WINDOWS)
POOL_STATE = max(POOL_WINDOWS) - 1
MLA_HEADS = 6
MLA_NOPE = 64
MLA_ROPE = 32
MLA_QK = MLA_NOPE + MLA_ROPE
MLA_V = 128
MLA_WIDTH = MLA_HEADS * MLA_V
Q_LORA = 384
KV_LORA = 128
MLA_CACHE = KV_LORA + MLA_ROPE
ROPE_THETA = 10000.0
NSA_HEADS = 12
NSA_DIM = 64
NSA_GROUPS = 2
NSA_REP = NSA_HEADS // NSA_GROUPS
NSA_WIDTH = NSA_HEADS * NSA_DIM
NSA_KV = NSA_GROUPS * NSA_DIM
NSA_BLOCK = 64
NSA_TOPN = 16
WINDOW = 512
MIX_WIDTH = POOL_WIDTH + MLA_WIDTH + NSA_WIDTH
IN_SPLITS = (POOL_WIDTH, Q_LORA, KV_LORA, MLA_ROPE, NSA_WIDTH) + (NSA_KV,) * 6 + (3 * NSA_HEADS,)
IN_COLS = sum(IN_SPLITS)
N_BUCKETS = 32
MAX_DISTANCE = 1024
D_FF = -(-8 * D_MODEL // (3 * 256)) * 256
Q_BLOCK = 128
EPS = 1e-6
NEG = -1e30

LANES = 128
KV_LANES = 2 * NSA_KV
POOL_TILE = 512
POOL_HALO = 16
VMEM_LIMIT_BYTES = 48 * 1024 * 1024
QT = 128
SEL_KT = 512
N_DELTA = 9
MLA_TQ = 512
MLA_CHUNK_PAGES = 16
NSA_CHUNK_PAGES = 16
ROWS_Q = NSA_REP * DEC_SEQ
MLA_ROWS_Q = MLA_HEADS * DEC_SEQ


def _mm_kernel(x_ref, w_ref, o_ref):
    o_ref[...] = jnp.dot(x_ref[...], w_ref[...], preferred_element_type=jnp.float32).astype(o_ref.dtype)


def _pmm(x, w, tm=512, tn=512, out_dtype=jnp.float32):
    M, K = x.shape
    N = w.shape[1]
    tm = min(tm, M)
    assert M % tm == 0
    n_pad = -(-N // tn) * tn
    xb = x.astype(jnp.bfloat16)
    wb = w.astype(jnp.bfloat16)
    if n_pad != N:
        wb = jnp.pad(wb, ((0, 0), (0, n_pad - N)))
    out = pl.pallas_call(
        _mm_kernel,
        grid=(M // tm, n_pad // tn),
        in_specs=[pl.BlockSpec((tm, K), lambda i, j: (i, 0)),
                  pl.BlockSpec((K, tn), lambda i, j: (0, j))],
        out_specs=pl.BlockSpec((tm, tn), lambda i, j: (i, j)),
        out_shape=jax.ShapeDtypeStruct((M, n_pad), out_dtype),
        compiler_params=pltpu.CompilerParams(
            dimension_semantics=("parallel", "parallel"), vmem_limit_bytes=VMEM_LIMIT_BYTES),
        name="dense_mm",
    )(xb, wb)
    return out[:, :N] if n_pad != N else out


def _gated_kernel(x_ref, wg_ref, wu_ref, o_ref):
    g = jnp.dot(x_ref[...], wg_ref[...], preferred_element_type=jnp.float32)
    u = jnp.dot(x_ref[...], wu_ref[...], preferred_element_type=jnp.float32)
    o_ref[...] = (g * jax.nn.sigmoid(g) * u).astype(o_ref.dtype)


def _mm_gated(x, wg, wu, tm=512, tn=512):
    M, K = x.shape
    N = wg.shape[1]
    tm = min(tm, M)
    assert M % tm == 0 and N % tn == 0
    bf = lambda a: a.astype(jnp.bfloat16)
    w_spec = pl.BlockSpec((K, tn), lambda i, j: (0, j))
    return pl.pallas_call(
        _gated_kernel,
        grid=(M // tm, N // tn),
        in_specs=[pl.BlockSpec((tm, K), lambda i, j: (i, 0)), w_spec, w_spec],
        out_specs=pl.BlockSpec((tm, tn), lambda i, j: (i, j)),
        out_shape=jax.ShapeDtypeStruct((M, N), jnp.bfloat16),
        compiler_params=pltpu.CompilerParams(
            dimension_semantics=("parallel", "parallel"), vmem_limit_bytes=VMEM_LIMIT_BYTES),
        name="ffn_gate_up",
    )(bf(x), bf(wg), bf(wu))


def _res_kernel(x_ref, w_ref, r_ref, o_ref):
    o_ref[...] = r_ref[...] + jnp.dot(x_ref[...], w_ref[...], preferred_element_type=jnp.float32)


def _mm_res(x, w, res, tm=512, tn=512):
    M, K = x.shape
    N = w.shape[1]
    tm = min(tm, M)
    assert M % tm == 0 and N % tn == 0
    return pl.pallas_call(
        _res_kernel,
        grid=(M // tm, N // tn),
        in_specs=[pl.BlockSpec((tm, K), lambda i, j: (i, 0)),
                  pl.BlockSpec((K, tn), lambda i, j: (0, j)),
                  pl.BlockSpec((tm, tn), lambda i, j: (i, j))],
        out_specs=pl.BlockSpec((tm, tn), lambda i, j: (i, j)),
        out_shape=jax.ShapeDtypeStruct((M, N), jnp.float32),
        compiler_params=pltpu.CompilerParams(
            dimension_semantics=("parallel", "parallel"), vmem_limit_bytes=VMEM_LIMIT_BYTES),
        name="dense_mm_residual",
    )(x.astype(jnp.bfloat16), w.astype(jnp.bfloat16), res)


def _mm(x, w, **kw):
    lead = x.shape[:-1]
    return _pmm(x.reshape(-1, x.shape[-1]), w, **kw).reshape(lead + (w.shape[1],))


def _rmsnorm(x, g):
    xf = x.astype(jnp.float32)
    y = xf * lax.rsqrt(jnp.mean(xf * xf, axis=-1, keepdims=True) + EPS)
    return (y * g.astype(jnp.float32)).astype(x.dtype)


def _rope_tables(n_pos):
    half = MLA_ROPE // 2
    inv = ROPE_THETA ** (-jnp.arange(half, dtype=jnp.float32) / half)
    ang = jnp.arange(n_pos).astype(jnp.float32)[:, None] * inv
    return jnp.cos(ang), jnp.sin(ang)


def _rope(x, pos):
    half = x.shape[-1] // 2
    inv = ROPE_THETA ** (-jnp.arange(half, dtype=jnp.float32) / half)
    ang = pos.astype(jnp.float32)[:, None] * inv
    cos, sin = jnp.cos(ang)[:, None, :], jnp.sin(ang)[:, None, :]
    xf = x.astype(jnp.float32)
    x1, x2 = xf[..., :half], xf[..., half:]
    return jnp.concatenate([x1 * cos - x2 * sin, x1 * sin + x2 * cos], axis=-1).astype(x.dtype)


def _rel_bucket(dist):
    n = jnp.maximum(dist, 0)
    exact = N_BUCKETS // 2
    nf = jnp.maximum(n, exact).astype(jnp.float32)
    large = exact + (jnp.log(nf / exact) / math.log(MAX_DISTANCE / exact) * (N_BUCKETS - exact)).astype(jnp.int32)
    return jnp.where(n < exact, n, jnp.minimum(large, N_BUCKETS - 1))


def _split_in(z):
    cuts = [int(c) for c in np.cumsum(IN_SPLITS)[:-1]]
    return jnp.split(z, cuts, axis=-1)


def _mla_q(q_lat, p, pos):
    q = _mm(_rmsnorm(q_lat, p['mla_q_norm']), p['w_uq'])
    q = _rmsnorm(q.reshape(q.shape[:-1] + (MLA_HEADS, MLA_QK)), p['mla_q_gain'])
    return jnp.concatenate([q[..., :MLA_NOPE], _rope(q[..., MLA_NOPE:], pos)], axis=-1)


def _mla_kv(rows, p, pos):
    kv = (rows[..., :KV_LORA] @ p['w_ukv']).reshape(rows.shape[:-1] + (MLA_HEADS, MLA_NOPE + MLA_V))
    k_pe = jnp.broadcast_to(rows[..., None, KV_LORA:], kv.shape[:-1] + (MLA_ROPE,))
    k = _rmsnorm(jnp.concatenate([kv[..., :MLA_NOPE], k_pe], axis=-1), p['mla_k_gain'])
    k = jnp.concatenate([k[..., :MLA_NOPE], _rope(k[..., MLA_NOPE:], pos)], axis=-1)
    return k, kv[..., MLA_NOPE:]


def _nsa_combine(gates, o_cmp, o_sel, o_win):
    g = gates[..., None]
    return g[:, :, 0] * o_cmp + g[:, :, 1] * o_sel + g[:, :, 2] * o_win


def _blocksum_kernel(k_ref, v_ref, gain_ref, kc_ref, vc_ref):
    nb = kc_ref.shape[1]
    k = k_ref[0].reshape(nb, NSA_BLOCK, LANES)
    v = v_ref[0].reshape(nb, NSA_BLOCK, LANES)
    km = jnp.sum(k, axis=1) * (1.0 / NSA_BLOCK)
    vm = jnp.sum(v, axis=1) * (1.0 / NSA_BLOCK)
    lo = lax.broadcasted_iota(jnp.int32, km.shape, 1) < NSA_DIM
    sq = km * km
    ss_lo = jnp.sum(jnp.where(lo, sq, 0.0), axis=-1, keepdims=True)
    ss_hi = jnp.sum(jnp.where(lo, 0.0, sq), axis=-1, keepdims=True)
    ss = jnp.where(lo, ss_lo, ss_hi)
    kc_ref[0] = km * lax.rsqrt(ss * (1.0 / NSA_DIM) + EPS) * gain_ref[...]
    vc_ref[0] = vm


def _block_summaries(k_cmp, v_cmp, kc_gain):
    B, T, _ = k_cmp.shape
    nb = T // NSA_BLOCK
    gain = jnp.tile(kc_gain.astype(jnp.float32), NSA_GROUPS).reshape(1, LANES)
    spec = pl.BlockSpec((1, T, LANES), lambda b: (b, 0, 0))
    ospec = pl.BlockSpec((1, nb, LANES), lambda b: (b, 0, 0))
    return pl.pallas_call(
        _blocksum_kernel,
        grid=(B,),
        in_specs=[spec, spec, pl.BlockSpec((1, LANES), lambda b: (0, 0))],
        out_specs=[ospec, ospec],
        out_shape=[jax.ShapeDtypeStruct((B, nb, LANES), jnp.float32)] * 2,
        compiler_params=pltpu.CompilerParams(dimension_semantics=("parallel",), vmem_limit_bytes=VMEM_LIMIT_BYTES),
        name="nsa_block_summaries",
    )(k_cmp, v_cmp, gain)


def _nsa_prompt_kernel(q_ref, kc_ref, vc_ref, bcmp_ref, ks_ref, vs_ref, kw_ref, vw_ref, btile_ref,
                       ocmp_ref, osel_ref, owin_ref, mexp_ref, m_ref, l_ref, acc_ref):
    qi = pl.program_id(1)
    T = ks_ref.shape[1]
    nb = kc_ref.shape[1]
    n_kt = T // QT
    scale = NSA_DIM ** -0.5
    rows = NSA_REP * QT

    tq = lax.broadcasted_iota(jnp.int32, (QT, nb), 0) + qi * QT
    nidx = lax.broadcasted_iota(jnp.int32, (QT, nb), 1)
    ii = lax.broadcasted_iota(jnp.int32, (QT, QT), 0)
    jj = lax.broadcasted_iota(jnp.int32, (QT, QT), 1)
    expand = (lax.broadcasted_iota(jnp.int32, (nb, T), 1) // NSA_BLOCK
              == lax.broadcasted_iota(jnp.int32, (nb, T), 0)).astype(jnp.bfloat16)
    lane_lo = lax.broadcasted_iota(jnp.int32, (QT, LANES), 1) < NSA_DIM

    def block_mask_bias(g, kb, windowed):
        delta = qi - kb
        dist = delta * QT + ii - jj
        if windowed:
            mask = (dist >= 0) & (dist < WINDOW)
        else:
            mask = (dist >= 0) & (mexp_ref[g, kb] > 0.5)
        dcl = jnp.clip(delta, 0, N_DELTA - 1)
        return mask, [btile_ref[g, r, dcl] for r in range(NSA_REP)]

    def tile_inputs(g, kb0, n_blk, windowed):
        parts = [block_mask_bias(g, kb0 + j, windowed) for j in range(n_blk)]
        mask = jnp.concatenate([p[0] for p in parts], axis=1)
        biases = [jnp.concatenate([p[1][r] for p in parts], axis=1) for r in range(NSA_REP)]
        return mask, biases

    def flash_sel(g, qb):
        m_ref[...] = jnp.full(m_ref.shape, NEG, jnp.float32)
        l_ref[...] = jnp.zeros(l_ref.shape, jnp.float32)
        acc_ref[...] = jnp.zeros(acc_ref.shape, jnp.float32)
        n_blk = SEL_KT // QT

        def body(kt, carry):
            k0 = pl.multiple_of(kt * SEL_KT, SEL_KT)
            k = ks_ref[0, pl.ds(k0, SEL_KT), :]
            v = vs_ref[0, pl.ds(k0, SEL_KT), :]
            s = lax.dot_general(qb, k, (((1,), (1,)), ((), ())), preferred_element_type=jnp.float32)
            mask, biases = tile_inputs(g, kt * n_blk, n_blk, False)
            ps, alphas = [], []
            for r in range(NSA_REP):
                sl = slice(r * QT, (r + 1) * QT)
                sr = jnp.where(mask, s[sl] + biases[r], NEG)
                m_old = m_ref[sl]
                m_new = jnp.maximum(m_old, jnp.max(sr, axis=-1, keepdims=True))
                p = jnp.exp(sr - m_new)
                alpha = jnp.exp(m_old - m_new)
                l_ref[sl] = alpha * l_ref[sl] + jnp.sum(p, axis=-1, keepdims=True)
                m_ref[sl] = m_new
                ps.append(p.astype(jnp.bfloat16))
                alphas.append(alpha)
            pv = jnp.dot(jnp.concatenate(ps, axis=0), v, preferred_element_type=jnp.float32)
            acc_ref[...] = jnp.concatenate(alphas, axis=0) * acc_ref[...] + pv
            return carry

        lax.fori_loop(0, qi // n_blk + 1, body, 0)
        return acc_ref[...] / jnp.maximum(l_ref[...], 1e-30)

    def window(g, qb):
        n_blk = WINDOW // QT + 1
        kb0 = jnp.maximum(qi - WINDOW // QT, 0)
        k0 = pl.multiple_of(kb0 * QT, QT)
        k = kw_ref[0, pl.ds(k0, n_blk * QT), :]
        v = vw_ref[0, pl.ds(k0, n_blk * QT), :]
        s = lax.dot_general(qb, k, (((1,), (1,)), ((), ())), preferred_element_type=jnp.float32)
        mask, biases = tile_inputs(g, kb0, n_blk, True)
        ps, ls = [], []
        for r in range(NSA_REP):
            sr = jnp.where(mask, s[r * QT:(r + 1) * QT] + biases[r], NEG)
            p = jnp.exp(sr - jnp.max(sr, axis=-1, keepdims=True))
            ls.append(jnp.sum(p, axis=-1, keepdims=True))
            ps.append(p.astype(jnp.bfloat16))
        pv = jnp.dot(jnp.concatenate(ps, axis=0), v, preferred_element_type=jnp.float32)
        return pv / jnp.maximum(jnp.concatenate(ls, axis=0), 1e-30)

    outs = []
    for g in range(NSA_GROUPS):
        qf = q_ref[0, g].reshape(rows, LANES)
        qb = (qf * scale).astype(jnp.bfloat16)
        s = lax.dot_general(qf, kc_ref[0], (((1,), (1,)), ((), ())),
                            precision=lax.Precision.HIGHEST, preferred_element_type=jnp.float32)
        s = s * scale + bcmp_ref[g].reshape(rows, nb)
        cmask = jnp.concatenate([tq >= nidx * NSA_BLOCK + NSA_BLOCK - 1] * NSA_REP, axis=0)
        cmaskf = cmask.astype(jnp.float32)
        s = jnp.where(cmask, s, NEG)
        p = jnp.exp(s - jnp.max(s, axis=-1, keepdims=True)) * cmaskf
        pr = p / jnp.maximum(jnp.sum(p, axis=-1, keepdims=True), 1e-30)
        o_cmp = jnp.dot(pr.astype(jnp.bfloat16), vc_ref[0].astype(jnp.bfloat16), preferred_element_type=jnp.float32)
        imp = pr[0:QT]
        for r in range(1, NSA_REP):
            imp = imp + pr[r * QT:(r + 1) * QT]
        cur = tq // NSA_BLOCK
        forced = (nidx == cur) | (nidx == 0)
        score = jnp.where(nidx <= cur, jnp.where(forced, -NEG, imp), NEG)
        cnt = jnp.zeros((QT, nb), jnp.float32)
        for j in range(nb):
            col = score[:, j:j + 1]
            beats = (col > score) | ((col == score) & (nidx > j))
            cnt = cnt + beats.astype(jnp.float32)
        msel = (cnt < float(min(NSA_TOPN, nb))).astype(jnp.bfloat16)
        mexp = jnp.dot(msel, expand, preferred_element_type=jnp.float32)
        for kk in range(n_kt):
            mexp_ref[g, kk] = mexp[:, kk * QT:(kk + 1) * QT]
        o_sel = flash_sel(g, qb)
        o_win = window(g, qb)
        outs.append((o_cmp, o_sel, o_win))

    for r in range(NSA_REP):
        sl = slice(r * QT, (r + 1) * QT)
        ocmp_ref[0, r] = jnp.where(lane_lo, outs[0][0][sl], outs[1][0][sl])
        osel_ref[0, r] = jnp.where(lane_lo, outs[0][1][sl], outs[1][1][sl])
        owin_ref[0, r] = jnp.where(lane_lo, outs[0][2][sl], outs[1][2][sl])


def _bucket_thresholds():
    b = _rel_bucket(jnp.arange(MAX_DISTANCE))
    return jnp.sum(b[None, :] < jnp.arange(1, N_BUCKETS)[:, None], axis=1)


def _bias_of_dist(bt, th, dist):
    ex = (slice(None), slice(None)) + (None,) * dist.ndim
    d = jnp.maximum(dist, 0)[None, None]
    val = jnp.broadcast_to(bt[0][ex], (NSA_GROUPS, NSA_REP) + dist.shape)
    for kk in range(1, N_BUCKETS):
        val = jnp.where(d >= th[kk - 1], bt[kk][ex], val)
    return val


def _nsa_prompt_tables(rel_bias, T):
    nb = T // NSA_BLOCK
    bt = rel_bias.astype(jnp.float32).reshape(N_BUCKETS, NSA_GROUPS, NSA_REP)
    th = _bucket_thresholds()
    d = (jnp.arange(N_DELTA)[:, None, None] * QT + jnp.arange(QT)[None, :, None] - jnp.arange(QT)[None, None, :])
    tiles = _bias_of_dist(bt, th, d)
    dc = jnp.arange(T)[:, None] - (jnp.arange(nb)[None, :] * NSA_BLOCK + NSA_BLOCK - 1)
    bcmp = _bias_of_dist(bt, th, dc)
    return tiles, bcmp


def _group_padded_queries(qt):
    z = jnp.zeros_like(qt[:, 0])
    return jnp.stack([jnp.concatenate([qt[:, 0], z], axis=-1), jnp.concatenate([z, qt[:, 1]], axis=-1)], axis=1)


def _nsa_prompt(q_n, k_cmp, v_cmp, k_sel, v_sel, k_win, v_win, kc_gain, tables):
    B, T = q_n.shape[:2]
    assert T % SEL_KT == 0 and SEL_KT % QT == 0 and QT == 2 * NSA_BLOCK and T >= WINDOW + QT
    nb = T // NSA_BLOCK
    tiles, bcmp = tables
    kc, vc = _block_summaries(k_cmp, v_cmp, kc_gain)
    qt = q_n.transpose(0, 2, 3, 1, 4).reshape(B, NSA_GROUPS, NSA_REP * T, NSA_DIM)
    q_pad = _group_padded_queries(qt).reshape(B, NSA_GROUPS, NSA_REP, T, LANES)
    bf = lambda a: a.astype(jnp.bfloat16)
    kv_spec = pl.BlockSpec((1, T, LANES), lambda b, i: (b, 0, 0))
    sm_spec = pl.BlockSpec((1, nb, LANES), lambda b, i: (b, 0, 0))
    o_spec = pl.BlockSpec((1, NSA_REP, QT, LANES), lambda b, i: (b, 0, i, 0))
    o_shape = jax.ShapeDtypeStruct((B, NSA_REP, T, LANES), jnp.float32)
    outs = pl.pallas_call(
        _nsa_prompt_kernel,
        grid=(B, T // QT),
        in_specs=[pl.BlockSpec((1, NSA_GROUPS, NSA_REP, QT, LANES), lambda b, i: (b, 0, 0, i, 0)),
                  sm_spec, sm_spec,
                  pl.BlockSpec((NSA_GROUPS, NSA_REP, QT, nb), lambda b, i: (0, 0, i, 0)),
                  kv_spec, kv_spec, kv_spec, kv_spec,
                  pl.BlockSpec((NSA_GROUPS, NSA_REP, N_DELTA, QT, QT), lambda b, i: (0, 0, 0, 0, 0))],
        out_specs=[o_spec, o_spec, o_spec],
        out_shape=[o_shape, o_shape, o_shape],
        scratch_shapes=[pltpu.VMEM((NSA_GROUPS, T // QT, QT, QT), jnp.float32),
                        pltpu.VMEM((NSA_REP * QT, 1), jnp.float32),
                        pltpu.VMEM((NSA_REP * QT, 1), jnp.float32),
                        pltpu.VMEM((NSA_REP * QT, LANES), jnp.float32)],
        compiler_params=pltpu.CompilerParams(
            dimension_semantics=("parallel", "arbitrary"), vmem_limit_bytes=VMEM_LIMIT_BYTES),
        name="nsa_prompt_attention",
    )(q_pad, kc, vc, bcmp, bf(k_sel), bf(v_sel), bf(k_win), bf(v_win), tiles)
    back = lambda o: o.reshape(B, NSA_REP, T, NSA_GROUPS, NSA_DIM).transpose(0, 2, 3, 1, 4)
    return tuple(back(o) for o in outs)


def _mla_prompt_kernel(q_ref, k_ref, v_ref, o_ref, m_ref, l_ref, acc_ref):
    qi = pl.program_id(2)
    tq = q_ref.shape[2]
    q = q_ref[0, 0]
    m_ref[...] = jnp.full(m_ref.shape, NEG, jnp.float32)
    l_ref[...] = jnp.zeros(l_ref.shape, jnp.float32)
    acc_ref[...] = jnp.zeros(acc_ref.shape, jnp.float32)

    def step(kt, diagonal):
        k0 = pl.multiple_of(kt * tq, tq)
        k = k_ref[0, 0, pl.ds(k0, tq), :]
        v = v_ref[0, 0, pl.ds(k0, tq), :]
        s = lax.dot_general(q, k, (((1,), (1,)), ((), ())), preferred_element_type=jnp.float32)
        if diagonal:
            ii = lax.broadcasted_iota(jnp.int32, s.shape, 0)
            jj = lax.broadcasted_iota(jnp.int32, s.shape, 1)
            s = jnp.where(jj <= ii, s, NEG)
        m_old = m_ref[...]
        m_new = jnp.maximum(m_old, jnp.max(s, axis=-1, keepdims=True))
        p = jnp.exp(s - m_new)
        alpha = jnp.exp(m_old - m_new)
        l_ref[...] = alpha * l_ref[...] + jnp.sum(p, axis=-1, keepdims=True)
        m_ref[...] = m_new
        acc_ref[...] = alpha * acc_ref[...] + jnp.dot(p.astype(jnp.bfloat16), v, preferred_element_type=jnp.float32)

    def body(kt, carry):
        step(kt, False)
        return carry

    lax.fori_loop(0, qi, body, 0)
    step(qi, True)
    o_ref[0, 0] = acc_ref[...] / l_ref[...]


def _mla_prompt(q, k, v):
    B, T, H, _ = q.shape
    assert T % MLA_TQ == 0
    scale = MLA_QK ** -0.5
    qt = (q * scale).astype(jnp.bfloat16).transpose(0, 2, 1, 3)
    kt = k.astype(jnp.bfloat16).transpose(0, 2, 1, 3)
    vt = v.astype(jnp.bfloat16).transpose(0, 2, 1, 3)
    out = pl.pallas_call(
        _mla_prompt_kernel,
        grid=(B, H, T // MLA_TQ),
        in_specs=[pl.BlockSpec((1, 1, MLA_TQ, MLA_QK), lambda b, h, i: (b, h, i, 0)),
                  pl.BlockSpec((1, 1, T, MLA_QK), lambda b, h, i: (b, h, 0, 0)),
                  pl.BlockSpec((1, 1, T, MLA_V), lambda b, h, i: (b, h, 0, 0))],
        out_specs=pl.BlockSpec((1, 1, MLA_TQ, MLA_V), lambda b, h, i: (b, h, i, 0)),
        out_shape=jax.ShapeDtypeStruct((B, H, T, MLA_V), jnp.float32),
        scratch_shapes=[pltpu.VMEM((MLA_TQ, 1), jnp.float32),
                        pltpu.VMEM((MLA_TQ, 1), jnp.float32),
                        pltpu.VMEM((MLA_TQ, MLA_V), jnp.float32)],
        compiler_params=pltpu.CompilerParams(
            dimension_semantics=("parallel", "parallel", "arbitrary"), vmem_limit_bytes=VMEM_LIMIT_BYTES),
        name="mla_prompt_attention",
    )(qt, kt, vt)
    return out.transpose(0, 2, 1, 3)


def _mla_sample_kernel(layer, n_pages, pt_ref, cache_ref, new_ref, qn_ref, qpe_ref, gn_ref, wuk_ref, wuv_ref,
                       tbl_ref, o_ref, buf_ref, sem_ref, m_ref, l_ref, acc_ref):
    b = pl.program_id(0)
    n_seq = pl.num_programs(0)
    cp = MLA_CHUNK_PAGES
    lc = cp * PAGE_SIZE
    n_chunks = n_pages // cp
    scale = MLA_QK ** -0.5

    def page_copy(seq, chunk, p, slot):
        page = pt_ref[seq, chunk * cp + p]
        return pltpu.make_async_copy(cache_ref.at[layer, page], buf_ref.at[slot, :, pl.ds(p * PAGE_SIZE, PAGE_SIZE)],
                                     sem_ref.at[slot])

    def start_chunk(seq, chunk, slot):
        for p in range(cp):
            page_copy(seq, chunk, p, slot).start()

    def wait_chunk(seq, chunk, slot):
        for p in range(cp):
            page_copy(seq, chunk, p, slot).wait()

    @pl.when(b == 0)
    def _():
        start_chunk(0, 0, 0)

    m_ref[...] = jnp.full(m_ref.shape, NEG, jnp.float32)
    l_ref[...] = jnp.zeros(l_ref.shape, jnp.float32)
    acc_ref[...] = jnp.zeros(acc_ref.shape, jnp.float32)

    qn = (qn_ref[0] * gn_ref[...]).astype(jnp.bfloat16)
    a_abs = jnp.dot(qn, wuk_ref[...], preferred_element_type=jnp.float32)
    lhs1 = jnp.concatenate([wuk_ref[...], a_abs.astype(jnp.bfloat16)], axis=0)
    qpe = qpe_ref[0].astype(jnp.bfloat16)
    n_kn = MLA_HEADS * MLA_NOPE

    def attend(rows_t, col0, mask):
        L = rows_t.shape[1]
        lat = rows_t[:KV_LORA].astype(jnp.bfloat16)
        kpe = rows_t[KV_LORA:]
        x = jnp.dot(lhs1, lat, preferred_element_type=jnp.float32)
        ss_pe = jnp.sum(kpe * kpe, axis=0, keepdims=True)
        f = (jnp.concatenate([kpe, kpe], axis=0) * tbl_ref[:, pl.ds(col0, L)]).astype(jnp.bfloat16)
        s_pe = jnp.dot(qpe, f, preferred_element_type=jnp.float32)
        ps, alphas = [], []
        for h in range(MLA_HEADS):
            kn = x[h * MLA_NOPE:(h + 1) * MLA_NOPE]
            ss = jnp.sum(kn * kn, axis=0, keepdims=True) + ss_pe
            r = lax.rsqrt(ss * (1.0 / MLA_QK) + EPS) * scale
            sl = slice(h * DEC_SEQ, (h + 1) * DEC_SEQ)
            s = (x[n_kn + h * DEC_SEQ:n_kn + (h + 1) * DEC_SEQ] + s_pe[sl]) * r
            if mask is not None:
                s = jnp.where(mask, s, NEG)
            m_old = m_ref[sl]
            m_new = jnp.maximum(m_old, jnp.max(s, axis=-1, keepdims=True))
            p = jnp.exp(s - m_new)
            if mask is not None:
                p = p * mask.astype(jnp.float32)
            alpha = jnp.exp(m_old - m_new)
            l_ref[sl] = alpha * l_ref[sl] + jnp.sum(p, axis=-1, keepdims=True)
            m_ref[sl] = m_new
            ps.append(p.astype(jnp.bfloat16))
            alphas.append(alpha)
        pv = lax.dot_general(jnp.concatenate(ps, axis=0), lat, (((1,), (1,)), ((), ())),
                             preferred_element_type=jnp.float32)
        acc_ref[...] = jnp.concatenate(alphas, axis=0) * acc_ref[...] + pv

    def chunk_body(c, carry):
        slot = c % 2

        @pl.when(c + 1 < n_chunks)
        def _():
            start_chunk(b, c + 1, 1 - slot)

        @pl.when((c + 1 == n_chunks) & (b + 1 < n_seq))
        def _():
            start_chunk(b + 1, 0, 1 - slot)

        wait_chunk(b, c, slot)
        attend(buf_ref[slot], pl.multiple_of(c * lc, lc), None)
        return carry

    lax.fori_loop(0, n_chunks, chunk_body, 0)

    tt = lax.broadcasted_iota(jnp.int32, (DEC_SEQ, PAGE_SIZE), 0)
    jj = lax.broadcasted_iota(jnp.int32, (DEC_SEQ, PAGE_SIZE), 1)
    attend(new_ref[0], n_pages * PAGE_SIZE, jj <= tt)

    o_lat = (acc_ref[...] / jnp.maximum(l_ref[...], 1e-30)).astype(jnp.bfloat16)
    for h in range(MLA_HEADS):
        o_ref[0, h * DEC_SEQ:(h + 1) * DEC_SEQ, :] = jnp.dot(o_lat[h * DEC_SEQ:(h + 1) * DEC_SEQ], wuv_ref[h],
                                                             preferred_element_type=jnp.float32)


def _mla_sample(q_m, rows_new, page_table, cache_mla_t, layer, w_ukv, k_gain, cos, sin):
    DB, S = q_m.shape[:2]
    n_pages = page_table.shape[1]
    assert S == DEC_SEQ and n_pages % MLA_CHUNK_PAGES == 0 and (n_pages // MLA_CHUNK_PAGES) % 2 == 0
    n_pos = n_pages * PAGE_SIZE + PAGE_SIZE
    f32 = jnp.float32
    half = MLA_ROPE // 2
    w = w_ukv.reshape(KV_LORA, MLA_HEADS, MLA_NOPE + MLA_V)
    w_uk = w[:, :, :MLA_NOPE].reshape(KV_LORA, MLA_HEADS * MLA_NOPE).T.astype(jnp.bfloat16)
    w_uv = w[:, :, MLA_NOPE:].transpose(1, 0, 2).astype(jnp.bfloat16)
    qt = q_m.transpose(0, 2, 1, 3)
    eye_h = jnp.eye(MLA_HEADS, dtype=f32)
    q_blk = (qt[:, :, :, None, :MLA_NOPE] * eye_h[None, :, None, :, None]).reshape(
        DB, MLA_ROWS_Q, MLA_HEADS * MLA_NOPE)
    q1, q2 = qt[..., MLA_NOPE:MLA_NOPE + half], qt[..., MLA_NOPE + half:]
    q_pe = jnp.concatenate([q1, q2, q2, -q1], axis=-1).reshape(DB, MLA_ROWS_Q, 2 * MLA_ROPE)
    gn = jnp.tile(k_gain[:MLA_NOPE].astype(f32), MLA_HEADS).reshape(1, MLA_HEADS * MLA_NOPE)
    g1, g2 = k_gain[MLA_NOPE:MLA_NOPE + half].astype(f32), k_gain[MLA_NOPE + half:].astype(f32)
    c, s = cos[:n_pos].T, sin[:n_pos].T
    tbl = jnp.concatenate([c * g1[:, None], c * g2[:, None], s * g1[:, None], s * g2[:, None]], axis=0)
    new_pad = jnp.pad(rows_new.transpose(0, 2, 1), ((0, 0), (0, 0), (0, PAGE_SIZE - S)))
    lc = MLA_CHUNK_PAGES * PAGE_SIZE
    const = lambda shape: pl.BlockSpec(shape, lambda b, pt: (0,) * len(shape))
    out = pl.pallas_call(
        functools.partial(_mla_sample_kernel, layer, n_pages),
        grid_spec=pltpu.PrefetchScalarGridSpec(
            num_scalar_prefetch=1,
            grid=(DB,),
            in_specs=[pl.BlockSpec(memory_space=pl.ANY),
                      pl.BlockSpec((1, MLA_CACHE, PAGE_SIZE), lambda b, pt: (b, 0, 0)),
                      pl.BlockSpec((1, MLA_ROWS_Q, MLA_HEADS * MLA_NOPE), lambda b, pt: (b, 0, 0)),
                      pl.BlockSpec((1, MLA_ROWS_Q, 2 * MLA_ROPE), lambda b, pt: (b, 0, 0)),
                      const((1, MLA_HEADS * MLA_NOPE)),
                      const((MLA_HEADS * MLA_NOPE, KV_LORA)),
                      const((MLA_HEADS, KV_LORA, MLA_V)),
                      const((2 * MLA_ROPE, n_pos))],
            out_specs=pl.BlockSpec((1, MLA_ROWS_Q, MLA_V), lambda b, pt: (b, 0, 0)),
            scratch_shapes=[pltpu.VMEM((2, MLA_CACHE, lc), f32),
                            pltpu.SemaphoreType.DMA((2,)),
                            pltpu.VMEM((MLA_ROWS_Q, 1), f32),
                            pltpu.VMEM((MLA_ROWS_Q, 1), f32),
                            pltpu.VMEM((MLA_ROWS_Q, KV_LORA), f32)]),
        out_shape=jax.ShapeDtypeStruct((DB, MLA_ROWS_Q, MLA_V), f32),
        compiler_params=pltpu.CompilerParams(dimension_semantics=("arbitrary",), vmem_limit_bytes=VMEM_LIMIT_BYTES),
        name="mla_sample_attention",
    )(page_table, cache_mla_t, new_pad, q_blk, q_pe, gn, w_uk, w_uv, tbl)
    return out.reshape(DB, MLA_HEADS, S, MLA_V).transpose(0, 2, 1, 3)


def _nsa_sample_kernel(layer, n_pages, pt_ref, ccmp_ref, csel_ref, q_ref, ncmp_ref, nsel_ref, nwin_ref, wbuf_ref,
                       gain_ref, bcmp_ref, bfar_ref, blast_ref, bnew_ref, bwin_ref, expand_ref, pool_ref,
                       ocmp_ref, osel_ref, owin_ref, buf_ref, sem_ref, sum_ref, msel_ref, m_ref, l_ref, acc_ref):
    b = pl.program_id(0)
    n_seq = pl.num_programs(0)
    cp = NSA_CHUNK_PAGES
    lc = cp * PAGE_SIZE
    n_chunks = n_pages // cp
    bpc = lc // NSA_BLOCK
    n_past = n_pages * PAGE_SIZE
    nbp = n_past // NSA_BLOCK
    nbs = sum_ref.shape[0]
    scale = NSA_DIM ** -0.5

    def page_copy(cache_ref, seq, chunk, p, slot):
        page = pt_ref[seq, chunk * cp + p]
        return pltpu.make_async_copy(cache_ref.at[layer, page], buf_ref.at[slot, :, pl.ds(p * PAGE_SIZE, PAGE_SIZE)],
                                     sem_ref.at[slot])

    def start_chunk(cache_ref, seq, chunk, slot):
        for p in range(cp):
            page_copy(cache_ref, seq, chunk, p, slot).start()

    def wait_chunk(cache_ref, seq, chunk, slot):
        for p in range(cp):
            page_copy(cache_ref, seq, chunk, p, slot).wait()

    @pl.when(b == 0)
    def _():
        start_chunk(ccmp_ref, 0, 0, 0)

    def cmp_body(c, carry):
        slot = c % 2

        @pl.when(c + 1 < n_chunks)
        def _():
            start_chunk(ccmp_ref, b, c + 1, 1 - slot)

        @pl.when(c + 1 == n_chunks)
        def _():
            start_chunk(csel_ref, b, 0, 1 - slot)

        wait_chunk(ccmp_ref, b, c, slot)
        x = buf_ref[slot]
        hi = x.astype(jnp.bfloat16)
        lo = (x - hi.astype(jnp.float32)).astype(jnp.bfloat16)
        nt = (((1,), (1,)), ((), ()))
        sum_ref[pl.ds(pl.multiple_of(c * bpc, bpc), bpc), :] = (
            lax.dot_general(pool_ref[...], hi, nt, preferred_element_type=jnp.float32)
            + lax.dot_general(pool_ref[...], lo, nt, preferred_element_type=jnp.float32))
        return carry

    lax.fori_loop(0, n_chunks, cmp_body, 0)
    pad_rows = nbs - nbp
    new_mean = jnp.sum(ncmp_ref[0], axis=0, keepdims=True) * (1.0 / NSA_BLOCK)
    row0 = lax.broadcasted_iota(jnp.int32, (pad_rows, KV_LANES), 0) == 0
    sum_ref[nbp:, :] = jnp.where(row0, jnp.broadcast_to(new_mean, (pad_rows, KV_LANES)), 0.0)

    summ = sum_ref[...]
    km, vm = summ[:, :LANES], summ[:, LANES:]
    lo_half = lax.broadcasted_iota(jnp.int32, km.shape, 1) < NSA_DIM
    sq = km * km
    ss = jnp.where(lo_half, jnp.sum(jnp.where(lo_half, sq, 0.0), axis=-1, keepdims=True),
                   jnp.sum(jnp.where(lo_half, 0.0, sq), axis=-1, keepdims=True))
    kc = km * lax.rsqrt(ss * (1.0 / NSA_DIM) + EPS) * gain_ref[...]
    vcb = vm.astype(jnp.bfloat16)

    tt = lax.broadcasted_iota(jnp.int32, (ROWS_Q, nbs), 0) % DEC_SEQ
    nn = lax.broadcasted_iota(jnp.int32, (ROWS_Q, nbs), 1)
    cmask = (n_past + tt) >= (nn * NSA_BLOCK + NSA_BLOCK - 1)
    cmaskf = cmask.astype(jnp.float32)
    zq = jnp.zeros((ROWS_Q, NSA_DIM), jnp.float32)
    imps = []
    for g in range(NSA_GROUPS):
        qg = q_ref[0, g]
        q_pad = jnp.concatenate([qg, zq] if g == 0 else [zq, qg], axis=1)
        s = lax.dot_general(q_pad, kc, (((1,), (1,)), ((), ())),
                            precision=lax.Precision.HIGHEST, preferred_element_type=jnp.float32)
        s = jnp.where(cmask, s * scale + bcmp_ref[g], NEG)
        p = jnp.exp(s - jnp.max(s, axis=-1, keepdims=True)) * cmaskf
        pr = p / jnp.maximum(jnp.sum(p, axis=-1, keepdims=True), 1e-30)
        o_full = jnp.dot(pr.astype(jnp.bfloat16), vcb, preferred_element_type=jnp.float32)
        ocmp_ref[0, g] = o_full[:, g * NSA_DIM:(g + 1) * NSA_DIM]
        imp = pr[0:DEC_SEQ]
        for r in range(1, NSA_REP):
            imp = imp + pr[r * DEC_SEQ:(r + 1) * DEC_SEQ]
        imps.append(imp)
    imp = jnp.concatenate(imps, axis=0)
    n16 = lax.broadcasted_iota(jnp.int32, imp.shape, 1)
    t16 = lax.broadcasted_iota(jnp.int32, imp.shape, 0) % DEC_SEQ
    cur = (n_past + t16) // NSA_BLOCK
    forced = (n16 == cur) | (n16 == 0)
    score = jnp.where(n16 <= cur, jnp.where(forced, -NEG, imp), NEG)
    cnt = jnp.zeros(imp.shape, jnp.float32)
    for j in range(nbp + 1):
        cj = score[:, j:j + 1]
        beats = (cj > score) | ((cj == score) & (n16 > j))
        cnt = cnt + beats.astype(jnp.float32)
    msel = (cnt < float(NSA_TOPN)).astype(jnp.float32)
    msel_ref[...] = jnp.zeros(msel_ref.shape, jnp.float32)
    for c in range(n_chunks):
        msel_ref[c, :, 0:bpc] = msel[:, c * bpc:(c + 1) * bpc]
    msel_new = msel[:, nbp:nbp + 1]

    def attend(g, kv_t, bias, mask):
        L = kv_t.shape[1]
        k = kv_t[g * NSA_DIM:(g + 1) * NSA_DIM].astype(jnp.bfloat16)
        v = kv_t[NSA_KV + g * NSA_DIM:NSA_KV + (g + 1) * NSA_DIM].astype(jnp.bfloat16)
        s = jnp.dot(qs[g], k, preferred_element_type=jnp.float32)
        if bias is not None:
            s = s + bias
        s = jnp.where(mask[None], s.reshape(NSA_REP, DEC_SEQ, L), NEG).reshape(ROWS_Q, L)
        m_old = m_ref[g]
        m_new = jnp.maximum(m_old, jnp.max(s, axis=-1, keepdims=True))
        p = jnp.exp(s - m_new)
        alpha = jnp.exp(m_old - m_new)
        l_ref[g] = alpha * l_ref[g] + jnp.sum(p, axis=-1, keepdims=True)
        m_ref[g] = m_new
        acc_ref[g] = alpha * acc_ref[g] + lax.dot_general(p.astype(jnp.bfloat16), v, (((1,), (1,)), ((), ())),
                                                          preferred_element_type=jnp.float32)

    def reset():
        m_ref[...] = jnp.full(m_ref.shape, NEG, jnp.float32)
        l_ref[...] = jnp.zeros(l_ref.shape, jnp.float32)
        acc_ref[...] = jnp.zeros(acc_ref.shape, jnp.float32)

    def result(g):
        return acc_ref[g] / l_ref[g]

    qs = [(q_ref[0, g] * scale).astype(jnp.bfloat16) for g in range(NSA_GROUPS)]
    tn = lax.broadcasted_iota(jnp.int32, (DEC_SEQ, PAGE_SIZE), 0)
    jn = lax.broadcasted_iota(jnp.int32, (DEC_SEQ, PAGE_SIZE), 1)
    new_causal = jn <= tn

    reset()

    def sel_chunk(c, slot, last):
        wait_chunk(csel_ref, b, c, slot)
        kv_t = buf_ref[slot]
        mexp = jnp.dot(msel_ref[c].astype(jnp.bfloat16), expand_ref[...], preferred_element_type=jnp.float32)
        for g in range(NSA_GROUPS):
            attend(g, kv_t, blast_ref[g] if last else None, mexp[g * DEC_SEQ:(g + 1) * DEC_SEQ] > 0.5)

    def sel_body(c, carry):
        slot = c % 2
        start_chunk(csel_ref, b, c + 1, 1 - slot)
        sel_chunk(c, slot, False)
        return carry

    lax.fori_loop(0, n_chunks - 1, sel_body, 0)

    @pl.when(b + 1 < n_seq)
    def _():
        start_chunk(ccmp_ref, b + 1, 0, n_chunks % 2)

    sel_chunk(n_chunks - 1, (n_chunks - 1) % 2, True)
    for g in range(NSA_GROUPS):
        attend(g, nsel_ref[0], bnew_ref[g] - bfar_ref[g], new_causal & (msel_new[g * DEC_SEQ:(g + 1) * DEC_SEQ] > 0.5))
        osel_ref[0, g] = result(g)

    reset()
    n_buf = wbuf_ref.shape[3]
    tw = lax.broadcasted_iota(jnp.int32, (DEC_SEQ, n_buf), 0)
    cw = lax.broadcasted_iota(jnp.int32, (DEC_SEQ, n_buf), 1)
    dist_w = n_buf + tw - cw
    wmask = (dist_w >= 0) & (dist_w < WINDOW)
    for g in range(NSA_GROUPS):
        attend(g, wbuf_ref[0, 0], bwin_ref[g], wmask)
        attend(g, nwin_ref[0], bnew_ref[g], new_causal)
        owin_ref[0, g] = result(g)


def _nsa_sample_tables(rel_bias, n_pages, n_buf):
    n_past = n_pages * PAGE_SIZE
    nbp = n_past // NSA_BLOCK
    nbs = -(-(nbp + 1) // LANES) * LANES
    lc = NSA_CHUNK_PAGES * PAGE_SIZE
    bt = rel_bias.astype(jnp.float32).reshape(N_BUCKETS, NSA_GROUPS, NSA_REP)
    th = _bucket_thresholds()
    t = jnp.arange(DEC_SEQ)

    def table(dist):
        return _bias_of_dist(bt, th, dist).reshape(NSA_GROUPS, ROWS_Q, dist.shape[1])

    bcmp = table(n_past + t[:, None] - (jnp.arange(nbs)[None, :] * NSA_BLOCK + NSA_BLOCK - 1))
    bfar = jnp.broadcast_to(bt[N_BUCKETS - 1][:, :, None, None], (NSA_GROUPS, NSA_REP, DEC_SEQ, 1)).reshape(
        NSA_GROUPS, ROWS_Q, 1)
    blast = table(n_past + t[:, None] - (n_past - lc + jnp.arange(lc))[None, :]) - bfar
    bnew = table(t[:, None] - jnp.arange(PAGE_SIZE)[None, :])
    bwin = table(n_buf + t[:, None] - jnp.arange(n_buf)[None, :])
    expand = (jnp.arange(lc)[None, :] // NSA_BLOCK == jnp.arange(LANES)[:, None]).astype(jnp.bfloat16)
    pool = (expand[:lc // NSA_BLOCK].astype(jnp.float32) * (1.0 / NSA_BLOCK)).astype(jnp.bfloat16)
    return bcmp, bfar, blast, bnew, bwin, expand, pool


def _rows_last(a):
    nd = a.ndim
    perm = tuple(range(nd - 4)) + (nd - 3, nd - 2, nd - 1, nd - 4)
    at = a.transpose(perm)
    return at.reshape(at.shape[:-4] + (KV_LANES, at.shape[-1]))


def _nsa_sample(q_n, kc_new, vc_new, ks_new, vs_new, kw_new, vw_new, page_table, cache_cmp_t, cache_sel_t, win_t,
                layer, kc_gain, tables):
    DB, S = q_n.shape[:2]
    n_pages = page_table.shape[1]
    n_buf = win_t.shape[3]
    lc = NSA_CHUNK_PAGES * PAGE_SIZE
    n_chunks = n_pages // NSA_CHUNK_PAGES
    assert S == DEC_SEQ and n_pages % NSA_CHUNK_PAGES == 0 and n_chunks % 2 == 0 and n_buf == WINDOW
    assert lc + 1 >= MAX_DISTANCE and lc // NSA_BLOCK <= LANES
    assert (n_pages * PAGE_SIZE) % NSA_BLOCK == 0 and S <= NSA_BLOCK
    f32 = jnp.float32
    bcmp, bfar, blast, bnew, bwin, expand, pool = tables
    nbs = bcmp.shape[-1]
    qt = q_n.transpose(0, 2, 3, 1, 4).reshape(DB, NSA_GROUPS, ROWS_Q, NSA_DIM)
    new_t = lambda k, v, width: jnp.pad(jnp.concatenate([k, v], axis=-1).transpose(0, 2, 1),
                                        ((0, 0), (0, 0), (0, width - S)))
    new_cmp = jnp.concatenate([kc_new, vc_new], axis=-1)
    new_sel = new_t(ks_new, vs_new, PAGE_SIZE)
    new_win = new_t(kw_new, vw_new, PAGE_SIZE)
    gain = jnp.tile(kc_gain.astype(f32), NSA_GROUPS).reshape(1, LANES)
    const = lambda shape: pl.BlockSpec(shape, lambda b, pt: (0,) * len(shape))
    per_seq = lambda shape: pl.BlockSpec((1,) + shape, lambda b, pt: (b,) + (0,) * len(shape))
    o_shape = jax.ShapeDtypeStruct((DB, NSA_GROUPS, ROWS_Q, NSA_DIM), f32)
    outs = pl.pallas_call(
        functools.partial(_nsa_sample_kernel, layer, n_pages),
        grid_spec=pltpu.PrefetchScalarGridSpec(
            num_scalar_prefetch=1,
            grid=(DB,),
            in_specs=[pl.BlockSpec(memory_space=pl.ANY), pl.BlockSpec(memory_space=pl.ANY),
                      per_seq((NSA_GROUPS, ROWS_Q, NSA_DIM)),
                      per_seq((S, KV_LANES)), per_seq((KV_LANES, PAGE_SIZE)), per_seq((KV_LANES, PAGE_SIZE)),
                      pl.BlockSpec((1, 1, KV_LANES, n_buf), lambda b, pt: (layer, b, 0, 0)),
                      const((1, LANES)),
                      const((NSA_GROUPS, ROWS_Q, nbs)), const((NSA_GROUPS, ROWS_Q, 1)),
                      const((NSA_GROUPS, ROWS_Q, lc)), const((NSA_GROUPS, ROWS_Q, PAGE_SIZE)),
                      const((NSA_GROUPS, ROWS_Q, n_buf)), const((LANES, lc)), const((lc // NSA_BLOCK, lc))],
            out_specs=[per_seq((NSA_GROUPS, ROWS_Q, NSA_DIM))] * 3,
            scratch_shapes=[pltpu.VMEM((2, KV_LANES, lc), f32),
                            pltpu.SemaphoreType.DMA((2,)),
                            pltpu.VMEM((nbs, KV_LANES), f32),
                            pltpu.VMEM((n_chunks, NSA_GROUPS * DEC_SEQ, LANES), f32),
                            pltpu.VMEM((NSA_GROUPS, ROWS_Q, 1), f32),
                            pltpu.VMEM((NSA_GROUPS, ROWS_Q, 1), f32),
                            pltpu.VMEM((NSA_GROUPS, ROWS_Q, NSA_DIM), f32)]),
        out_shape=[o_shape] * 3,
        compiler_params=pltpu.CompilerParams(dimension_semantics=("arbitrary",), vmem_limit_bytes=VMEM_LIMIT_BYTES),
        name="nsa_sample_attention",
    )(page_table, cache_cmp_t, cache_sel_t, qt, new_cmp, new_sel, new_win, win_t, gain,
      bcmp, bfar, blast, bnew, bwin, expand, pool)
    back = lambda o: o.reshape(DB, NSA_GROUPS, NSA_REP, S, NSA_DIM).transpose(0, 3, 1, 2, 4)
    return tuple(back(o) for o in outs)


def _pool_kernel(pos0, halo_ref, u_ref, w_ref, scale_ref, o_ref, ext_ref):
    i = pl.program_id(1)
    tile = u_ref.shape[1]

    @pl.when(i == 0)
    def _():
        ext_ref[0:POOL_HALO] = halo_ref[0]

    @pl.when(i > 0)
    def _():
        ext_ref[0:POOL_HALO] = ext_ref[tile:tile + POOL_HALO]

    ext_ref[POOL_HALO:POOL_HALO + tile] = u_ref[0]
    pos = pos0 + i * tile + lax.broadcasted_iota(jnp.int32, (tile, 1), 0)
    for gi, w in enumerate(POOL_WINDOWS):
        cols = slice(gi * POOL_GROUP, (gi + 1) * POOL_GROUP)
        x = ext_ref[POOL_HALO:POOL_HALO + tile, cols]
        win_sum = x
        for k in range(1, w):
            win_sum = win_sum + ext_ref[POOL_HALO - k:POOL_HALO - k + tile, cols]
        cnt = jnp.minimum(w, pos + 1).astype(jnp.float32)
        d = win_sum / cnt - x
        y = jnp.dot(d.astype(jnp.bfloat16), w_ref[gi], preferred_element_type=jnp.float32)
        o_ref[0, :, cols] = y * scale_ref[:, cols]


def _pool_mix(u, halo, pos0, w_pool, scale, tile):
    N, L, _ = u.shape
    assert L % tile == 0 and tile % 8 == 0 and (tile >= POOL_HALO or L == tile)
    return pl.pallas_call(
        functools.partial(_pool_kernel, pos0),
        grid=(N, L // tile),
        in_specs=[pl.BlockSpec((1, POOL_HALO, POOL_WIDTH), lambda n, i: (n, 0, 0)),
                  pl.BlockSpec((1, tile, POOL_WIDTH), lambda n, i: (n, i, 0)),
                  pl.BlockSpec((len(POOL_WINDOWS), POOL_GROUP, POOL_GROUP), lambda n, i: (0, 0, 0)),
                  pl.BlockSpec((1, POOL_WIDTH), lambda n, i: (0, 0))],
        out_specs=pl.BlockSpec((1, tile, POOL_WIDTH), lambda n, i: (n, i, 0)),
        out_shape=jax.ShapeDtypeStruct((N, L, POOL_WIDTH), jnp.float32),
        scratch_shapes=[pltpu.VMEM((POOL_HALO + tile, POOL_WIDTH), jnp.float32)],
        compiler_params=pltpu.CompilerParams(
            dimension_semantics=("parallel", "arbitrary"), vmem_limit_bytes=VMEM_LIMIT_BYTES),
        name="pool_mix",
    )(halo, u, w_pool.astype(jnp.bfloat16), scale.astype(jnp.float32).reshape(1, POOL_WIDTH))


def _project(x, pos, p):
    z = _mm(_rmsnorm(x, p['attn_norm']), p['w_in'], tn=384)
    (u_pool, q_lat, kv_lat, k_pe, q_nsa, k_cmp, v_cmp, k_sel, v_sel, k_win, v_win, g_logit) = _split_in(z)
    N, L = x.shape[:2]
    grp = lambda a: a.reshape(N, L, NSA_GROUPS, NSA_DIM)
    q_mla = _mla_q(q_lat, p, pos)
    mla_rows = jnp.concatenate([_rmsnorm(kv_lat, p['mla_kv_norm']), k_pe], axis=-1)
    q_n = _rmsnorm(q_nsa.reshape(N, L, NSA_GROUPS, NSA_REP, NSA_DIM), p['nsa_q_gain'])
    k_sel = _rmsnorm(grp(k_sel), p['nsa_k_gain'][1])
    k_win = _rmsnorm(grp(k_win), p['nsa_k_gain'][2])
    gates = jax.nn.sigmoid(g_logit).reshape(N, L, 3, NSA_GROUPS, NSA_REP)
    return u_pool, q_mla, mla_rows, q_n, grp(k_cmp), grp(v_cmp), k_sel, grp(v_sel), k_win, grp(v_win), gates


def _finish(x, pool_o, mla_o, nsa_o, p):
    N, L = x.shape[:2]
    mix = jnp.concatenate([pool_o, mla_o.reshape(N, L, MLA_WIDTH), nsa_o.reshape(N, L, NSA_WIDTH)], axis=-1)
    x2 = _mm_res(mix.reshape(N * L, MIX_WIDTH), p['w_out'], x.reshape(N * L, D_MODEL))
    h = _rmsnorm(x2, p['ffn_norm'])
    a = _mm_gated(h, p['w_gate'], p['w_up'])
    return _mm_res(a, p['w_down'], x2).reshape(N, L, D_MODEL)


def kernel(x_prompt, x_sample, cache_mla, cache_nsa_cmp, cache_nsa_sel, state_nsa_win, state_pool, page_table,
           rel_bias, attn_norm, w_in, pool_w, pool_scale, mla_q_norm, mla_kv_norm, w_uq, w_ukv,
           mla_q_gain, mla_k_gain, nsa_q_gain, nsa_k_gain, w_out, ffn_norm, w_gate, w_up, w_down):
    xp, xs = x_prompt, x_sample
    B, T = xp.shape[:2]
    DB, S = xs.shape[:2]
    n_pages = page_table.shape[1]
    n_buf = state_nsa_win.shape[2]
    n_ctx = state_pool.shape[2]
    pos_p = jnp.arange(T)
    pos_s = PAST_LEN + jnp.arange(S)
    cos, sin = _rope_tables(n_pages * PAGE_SIZE + PAGE_SIZE)
    prompt_tables = _nsa_prompt_tables(rel_bias, T)
    sample_tables = _nsa_sample_tables(rel_bias, n_pages, n_buf)
    cache_mla_t = cache_mla.transpose(0, 1, 3, 2)
    cache_cmp_t = _rows_last(cache_nsa_cmp)
    cache_sel_t = _rows_last(cache_nsa_sel)
    win_state_t = _rows_last(state_nsa_win)
    pool_halo_p = jnp.zeros((B, POOL_HALO, POOL_WIDTH), jnp.float32)
    pool_halo_s = jnp.pad(state_pool, ((0, 0), (0, 0), (POOL_HALO - n_ctx, 0), (0, 0)))
    flat = lambda a: a.reshape(a.shape[:2] + (NSA_KV,))
    nmla_p, ncmp_p, nsel_p, nwin_p, npool_p = [], [], [], [], []
    nmla_s, ncmp_s, nsel_s, nwin_s, npool_s = [], [], [], [], []
    for l in range(DEPTH):
        p = {'attn_norm': attn_norm[l], 'w_in': w_in[l], 'mla_q_norm': mla_q_norm[l],
             'mla_kv_norm': mla_kv_norm[l], 'w_uq': w_uq[l], 'w_ukv': w_ukv[l],
             'mla_q_gain': mla_q_gain[l], 'mla_k_gain': mla_k_gain[l], 'nsa_q_gain': nsa_q_gain[l],
             'nsa_k_gain': nsa_k_gain[l], 'w_out': w_out[l], 'ffn_norm': ffn_norm[l],
             'w_gate': w_gate[l], 'w_up': w_up[l], 'w_down': w_down[l]}

        u, q_m, rows_m, q_n, kc, vc, ks, vs, kw, vw, gates = _project(xp, pos_p, p)
        pool_o = _pool_mix(u, pool_halo_p, 0, pool_w[l], pool_scale[l], POOL_TILE)
        k_m, v_m = _mla_kv(rows_m, p, pos_p)
        mla_o = _mla_prompt(q_m, k_m, v_m)
        o_cmp, o_sel, o_win = _nsa_prompt(q_n, flat(kc), flat(vc), flat(ks), flat(vs), flat(kw), flat(vw),
                                          p['nsa_k_gain'][0], prompt_tables)
        nsa_o = _nsa_combine(gates, o_cmp, o_sel, o_win)
        xp = _finish(xp, pool_o, mla_o, nsa_o, p)
        nmla_p.append(rows_m)
        ncmp_p.append(jnp.stack([kc, vc], axis=2))
        nsel_p.append(jnp.stack([ks, vs], axis=2))
        nwin_p.append(jnp.stack([kw, vw], axis=2)[:, T - min(WINDOW, T):])
        npool_p.append(u[:, T - POOL_STATE:])

        u, q_m, rows_m, q_n, kc, vc, ks, vs, kw, vw, gates = _project(xs, pos_s, p)
        u_ext = jnp.concatenate([state_pool[l], u], axis=1)
        pool_o = _pool_mix(u, pool_halo_s[l], PAST_LEN, pool_w[l], pool_scale[l], S)
        mla_o = _mla_sample(q_m, rows_m, page_table, cache_mla_t, l, p['w_ukv'], p['mla_k_gain'], cos, sin)
        o_cmp, o_sel, o_win = _nsa_sample(q_n, flat(kc), flat(vc), flat(ks), flat(vs), flat(kw), flat(vw),
                                          page_table, cache_cmp_t, cache_sel_t, win_state_t, l,
                                          p['nsa_k_gain'][0], sample_tables)
        nsa_o = _nsa_combine(gates, o_cmp, o_sel, o_win)
        xs = _finish(xs, pool_o, mla_o, nsa_o, p)
        nmla_s.append(rows_m)
        ncmp_s.append(jnp.stack([kc, vc], axis=2))
        nsel_s.append(jnp.stack([ks, vs], axis=2))
        nwin_s.append(jnp.concatenate([state_nsa_win[l], jnp.stack([kw, vw], axis=2)], axis=1)[:, S:S + n_buf])
        npool_s.append(u_ext[:, S:S + n_ctx])

    return (xp, xs,
            jnp.stack(nmla_p), jnp.stack(ncmp_p), jnp.stack(nsel_p), jnp.stack(nwin_p), jnp.stack(npool_p),
            jnp.stack(nmla_s), jnp.stack(ncmp_s), jnp.stack(nsel_s), jnp.stack(nwin_s), jnp.stack(npool_s))
```

```python
import functools
import math

import jax
import jax.numpy as jnp
import numpy as np
from jax import lax
from jax.experimental import pallas as pl
from jax.experimental.pallas import tpu as pltpu

D_MODEL = 2048
BATCH = 2
SEQ = 4096
DEPTH = 2
DEC_BATCH = 128
DEC_SEQ = 8
PAST_LEN = 16384
PAGE_SIZE = 128
POOL_WINDOWS = (2, 4, 8, 16)
POOL_WIDTH = D_MODEL // 4
POOL_GROUP = POOL_WIDTH // len(POOL_WINDOWS)
POOL_STATE = max(POOL_WINDOWS) - 1
MLA_HEADS = 6
MLA_NOPE = 64
MLA_ROPE = 32
MLA_QK = MLA_NOPE + MLA_ROPE
MLA_V = 128
MLA_WIDTH = MLA_HEADS * MLA_V
Q_LORA = 384
KV_LORA = 128
MLA_CACHE = KV_LORA + MLA_ROPE
ROPE_THETA = 10000.0
NSA_HEADS = 12
NSA_DIM = 64
NSA_GROUPS = 2
NSA_REP = NSA_HEADS // NSA_GROUPS
NSA_WIDTH = NSA_HEADS * NSA_DIM
NSA_KV = NSA_GROUPS * NSA_DIM
NSA_BLOCK = 64
NSA_TOPN = 16
WINDOW = 512
MIX_WIDTH = POOL_WIDTH + MLA_WIDTH + NSA_WIDTH
IN_SPLITS = (POOL_WIDTH, Q_LORA, KV_LORA, MLA_ROPE, NSA_WIDTH) + (NSA_KV,) * 6 + (3 * NSA_HEADS,)
IN_COLS = sum(IN_SPLITS)
N_BUCKETS = 32
MAX_DISTANCE = 1024
D_FF = -(-8 * D_MODEL // (3 * 256)) * 256
Q_BLOCK = 128
EPS = 1e-6
NEG = -1e30

LANES = 128
KV_LANES = 2 * NSA_KV
POOL_TILE = 512
POOL_HALO = 16
VMEM_LIMIT_BYTES = 48 * 1024 * 1024
QT = 128
SEL_KT = 512
N_DELTA = 9
MLA_TQ = 512
MLA_CHUNK_PAGES = 32
NSA_CHUNK_PAGES = 64
ROWS_Q = NSA_REP * DEC_SEQ
MLA_ROWS_Q = MLA_HEADS * DEC_SEQ


def _mm_kernel(x_ref, w_ref, o_ref):
    o_ref[...] = jnp.dot(x_ref[...], w_ref[...], preferred_element_type=jnp.float32).astype(o_ref.dtype)


def _pmm(x, w, tm=512, tn=512, out_dtype=jnp.float32):
    M, K = x.shape
    N = w.shape[1]
    tm = min(tm, M)
    assert M % tm == 0
    n_pad = -(-N // tn) * tn
    xb = x.astype(jnp.bfloat16)
    wb = w.astype(jnp.bfloat16)
    if n_pad != N:
        wb = jnp.pad(wb, ((0, 0), (0, n_pad - N)))
    out = pl.pallas_call(
        _mm_kernel,
        grid=(M // tm, n_pad // tn),
        in_specs=[pl.BlockSpec((tm, K), lambda i, j: (i, 0)),
                  pl.BlockSpec((K, tn), lambda i, j: (0, j))],
        out_specs=pl.BlockSpec((tm, tn), lambda i, j: (i, j)),
        out_shape=jax.ShapeDtypeStruct((M, n_pad), out_dtype),
        compiler_params=pltpu.CompilerParams(
            dimension_semantics=("parallel", "parallel"), vmem_limit_bytes=VMEM_LIMIT_BYTES),
        name="dense_mm",
    )(xb, wb)
    return out[:, :N] if n_pad != N else out


def _gated_kernel(x_ref, wg_ref, wu_ref, o_ref):
    g = jnp.dot(x_ref[...], wg_ref[...], preferred_element_type=jnp.float32)
    u = jnp.dot(x_ref[...], wu_ref[...], preferred_element_type=jnp.float32)
    o_ref[...] = (g * jax.nn.sigmoid(g) * u).astype(o_ref.dtype)


def _mm_gated(x, wg, wu, tm=512, tn=512):
    M, K = x.shape
    N = wg.shape[1]
    tm = min(tm, M)
    assert M % tm == 0 and N % tn == 0
    bf = lambda a: a.astype(jnp.bfloat16)
    w_spec = pl.BlockSpec((K, tn), lambda i, j: (0, j))
    return pl.pallas_call(
        _gated_kernel,
        grid=(M // tm, N // tn),
        in_specs=[pl.BlockSpec((tm, K), lambda i, j: (i, 0)), w_spec, w_spec],
        out_specs=pl.BlockSpec((tm, tn), lambda i, j: (i, j)),
        out_shape=jax.ShapeDtypeStruct((M, N), jnp.bfloat16),
        compiler_params=pltpu.CompilerParams(
            dimension_semantics=("parallel", "parallel"), vmem_limit_bytes=VMEM_LIMIT_BYTES),
        name="ffn_gate_up",
    )(bf(x), bf(wg), bf(wu))


def _res_kernel(x_ref, w_ref, r_ref, o_ref):
    o_ref[...] = r_ref[...] + jnp.dot(x_ref[...], w_ref[...], preferred_element_type=jnp.float32)


def _mm_res(x, w, res, tm=512, tn=512):
    M, K = x.shape
    N = w.shape[1]
    tm = min(tm, M)
    assert M % tm == 0 and N % tn == 0
    return pl.pallas_call(
        _res_kernel,
        grid=(M // tm, N // tn),
        in_specs=[pl.BlockSpec((tm, K), lambda i, j: (i, 0)),
                  pl.BlockSpec((K, tn), lambda i, j: (0, j)),
                  pl.BlockSpec((tm, tn), lambda i, j: (i, j))],
        out_specs=pl.BlockSpec((tm, tn), lambda i, j: (i, j)),
        out_shape=jax.ShapeDtypeStruct((M, N), jnp.float32),
        compiler_params=pltpu.CompilerParams(
            dimension_semantics=("parallel", "parallel"), vmem_limit_bytes=VMEM_LIMIT_BYTES),
        name="dense_mm_residual",
    )(x.astype(jnp.bfloat16), w.astype(jnp.bfloat16), res)


def _mm(x, w, **kw):
    lead = x.shape[:-1]
    return _pmm(x.reshape(-1, x.shape[-1]), w, **kw).reshape(lead + (w.shape[1],))


def _rmsnorm(x, g):
    xf = x.astype(jnp.float32)
    y = xf * lax.rsqrt(jnp.mean(xf * xf, axis=-1, keepdims=True) + EPS)
    return (y * g.astype(jnp.float32)).astype(x.dtype)


def _rope_tables(n_pos):
    half = MLA_ROPE // 2
    inv = ROPE_THETA ** (-jnp.arange(half, dtype=jnp.float32) / half)
    ang = jnp.arange(n_pos).astype(jnp.float32)[:, None] * inv
    return jnp.cos(ang), jnp.sin(ang)


def _rope(x, pos):
    half = x.shape[-1] // 2
    inv = ROPE_THETA ** (-jnp.arange(half, dtype=jnp.float32) / half)
    ang = pos.astype(jnp.float32)[:, None] * inv
    cos, sin = jnp.cos(ang)[:, None, :], jnp.sin(ang)[:, None, :]
    xf = x.astype(jnp.float32)
    x1, x2 = xf[..., :half], xf[..., half:]
    return jnp.concatenate([x1 * cos - x2 * sin, x1 * sin + x2 * cos], axis=-1).astype(x.dtype)


def _rel_bucket(dist):
    n = jnp.maximum(dist, 0)
    exact = N_BUCKETS // 2
    nf = jnp.maximum(n, exact).astype(jnp.float32)
    large = exact + (jnp.log(nf / exact) / math.log(MAX_DISTANCE / exact) * (N_BUCKETS - exact)).astype(jnp.int32)
    return jnp.where(n < exact, n, jnp.minimum(large, N_BUCKETS - 1))


def _split_in(z):
    cuts = [int(c) for c in np.cumsum(IN_SPLITS)[:-1]]
    return jnp.split(z, cuts, axis=-1)


def _mla_q(q_lat, p, pos):
    q = _mm(_rmsnorm(q_lat, p['mla_q_norm']), p['w_uq'])
    q = _rmsnorm(q.reshape(q.shape[:-1] + (MLA_HEADS, MLA_QK)), p['mla_q_gain'])
    return jnp.concatenate([q[..., :MLA_NOPE], _rope(q[..., MLA_NOPE:], pos)], axis=-1)


def _mla_kv(rows, p, pos):
    kv = (rows[..., :KV_LORA] @ p['w_ukv']).reshape(rows.shape[:-1] + (MLA_HEADS, MLA_NOPE + MLA_V))
    k_pe = jnp.broadcast_to(rows[..., None, KV_LORA:], kv.shape[:-1] + (MLA_ROPE,))
    k = _rmsnorm(jnp.concatenate([kv[..., :MLA_NOPE], k_pe], axis=-1), p['mla_k_gain'])
    k = jnp.concatenate([k[..., :MLA_NOPE], _rope(k[..., MLA_NOPE:], pos)], axis=-1)
    return k, kv[..., MLA_NOPE:]


def _nsa_combine(gates, o_cmp, o_sel, o_win):
    g = gates[..., None]
    return g[:, :, 0] * o_cmp + g[:, :, 1] * o_sel + g[:, :, 2] * o_win


def _blocksum_kernel(k_ref, v_ref, gain_ref, kc_ref, vc_ref):
    nb = kc_ref.shape[1]
    k = k_ref[0].reshape(nb, NSA_BLOCK, LANES)
    v = v_ref[0].reshape(nb, NSA_BLOCK, LANES)
    km = jnp.sum(k, axis=1) * (1.0 / NSA_BLOCK)
    vm = jnp.sum(v, axis=1) * (1.0 / NSA_BLOCK)
    lo = lax.broadcasted_iota(jnp.int32, km.shape, 1) < NSA_DIM
    sq = km * km
    ss_lo = jnp.sum(jnp.where(lo, sq, 0.0), axis=-1, keepdims=True)
    ss_hi = jnp.sum(jnp.where(lo, 0.0, sq), axis=-1, keepdims=True)
    ss = jnp.where(lo, ss_lo, ss_hi)
    kc_ref[0] = km * lax.rsqrt(ss * (1.0 / NSA_DIM) + EPS) * gain_ref[...]
    vc_ref[0] = vm


def _block_summaries(k_cmp, v_cmp, kc_gain):
    B, T, _ = k_cmp.shape
    nb = T // NSA_BLOCK
    gain = jnp.tile(kc_gain.astype(jnp.float32), NSA_GROUPS).reshape(1, LANES)
    spec = pl.BlockSpec((1, T, LANES), lambda b: (b, 0, 0))
    ospec = pl.BlockSpec((1, nb, LANES), lambda b: (b, 0, 0))
    return pl.pallas_call(
        _blocksum_kernel,
        grid=(B,),
        in_specs=[spec, spec, pl.BlockSpec((1, LANES), lambda b: (0, 0))],
        out_specs=[ospec, ospec],
        out_shape=[jax.ShapeDtypeStruct((B, nb, LANES), jnp.float32)] * 2,
        compiler_params=pltpu.CompilerParams(dimension_semantics=("parallel",), vmem_limit_bytes=VMEM_LIMIT_BYTES),
        name="nsa_block_summaries",
    )(k_cmp, v_cmp, gain)


def _nsa_prompt_kernel(q_ref, kc_ref, vc_ref, bcmp_ref, ks_ref, vs_ref, kw_ref, vw_ref, btile_ref,
                       ocmp_ref, osel_ref, owin_ref, mexp_ref, m_ref, l_ref, acc_ref):
    qi = pl.program_id(1)
    T = ks_ref.shape[1]
    nb = kc_ref.shape[1]
    n_kt = T // QT
    scale = NSA_DIM ** -0.5
    rows = NSA_REP * QT

    tq = lax.broadcasted_iota(jnp.int32, (QT, nb), 0) + qi * QT
    nidx = lax.broadcasted_iota(jnp.int32, (QT, nb), 1)
    ii = lax.broadcasted_iota(jnp.int32, (QT, QT), 0)
    jj = lax.broadcasted_iota(jnp.int32, (QT, QT), 1)
    expand = (lax.broadcasted_iota(jnp.int32, (nb, T), 1) // NSA_BLOCK
              == lax.broadcasted_iota(jnp.int32, (nb, T), 0)).astype(jnp.bfloat16)
    lane_lo = lax.broadcasted_iota(jnp.int32, (QT, LANES), 1) < NSA_DIM

    def block_mask_bias(g, kb, windowed):
        delta = qi - kb
        dist = delta * QT + ii - jj
        if windowed:
            mask = (dist >= 0) & (dist < WINDOW)
        else:
            mask = (dist >= 0) & (mexp_ref[g, kb] > 0.5)
        dcl = jnp.clip(delta, 0, N_DELTA - 1)
        return mask, [btile_ref[g, r, dcl] for r in range(NSA_REP)]

    def tile_inputs(g, kb0, n_blk, windowed):
        parts = [block_mask_bias(g, kb0 + j, windowed) for j in range(n_blk)]
        mask = jnp.concatenate([p[0] for p in parts], axis=1)
        biases = [jnp.concatenate([p[1][r] for p in parts], axis=1) for r in range(NSA_REP)]
        return mask, biases

    def flash_sel(g, qb):
        m_ref[...] = jnp.full(m_ref.shape, NEG, jnp.float32)
        l_ref[...] = jnp.zeros(l_ref.shape, jnp.float32)
        acc_ref[...] = jnp.zeros(acc_ref.shape, jnp.float32)
        n_blk = SEL_KT // QT

        def body(kt, carry):
            k0 = pl.multiple_of(kt * SEL_KT, SEL_KT)
            k = ks_ref[0, pl.ds(k0, SEL_KT), :]
            v = vs_ref[0, pl.ds(k0, SEL_KT), :]
            s = lax.dot_general(qb, k, (((1,), (1,)), ((), ())), preferred_element_type=jnp.float32)
            mask, biases = tile_inputs(g, kt * n_blk, n_blk, False)
            ps, alphas = [], []
            for r in range(NSA_REP):
                sl = slice(r * QT, (r + 1) * QT)
                sr = jnp.where(mask, s[sl] + biases[r], NEG)
                m_old = m_ref[sl]
                m_new = jnp.maximum(m_old, jnp.max(sr, axis=-1, keepdims=True))
                p = jnp.exp(sr - m_new)
                alpha = jnp.exp(m_old - m_new)
                l_ref[sl] = alpha * l_ref[sl] + jnp.sum(p, axis=-1, keepdims=True)
                m_ref[sl] = m_new
                ps.append(p.astype(jnp.bfloat16))
                alphas.append(alpha)
            pv = jnp.dot(jnp.concatenate(ps, axis=0), v, preferred_element_type=jnp.float32)
            acc_ref[...] = jnp.concatenate(alphas, axis=0) * acc_ref[...] + pv
            return carry

        lax.fori_loop(0, qi // n_blk + 1, body, 0)
        return acc_ref[...] / jnp.maximum(l_ref[...], 1e-30)

    def window(g, qb):
        n_blk = WINDOW // QT + 1
        kb0 = jnp.maximum(qi - WINDOW // QT, 0)
        k0 = pl.multiple_of(kb0 * QT, QT)
        k = kw_ref[0, pl.ds(k0, n_blk * QT), :]
        v = vw_ref[0, pl.ds(k0, n_blk * QT), :]
        s = lax.dot_general(qb, k, (((1,), (1,)), ((), ())), preferred_element_type=jnp.float32)
        mask, biases = tile_inputs(g, kb0, n_blk, True)
        ps, ls = [], []
        for r in range(NSA_REP):
            sr = jnp.where(mask, s[r * QT:(r + 1) * QT] + biases[r], NEG)
            p = jnp.exp(sr - jnp.max(sr, axis=-1, keepdims=True))
            ls.append(jnp.sum(p, axis=-1, keepdims=True))
            ps.append(p.astype(jnp.bfloat16))
        pv = jnp.dot(jnp.concatenate(ps, axis=0), v, preferred_element_type=jnp.float32)
        return pv / jnp.maximum(jnp.concatenate(ls, axis=0), 1e-30)

    outs = []
    for g in range(NSA_GROUPS):
        qf = q_ref[0, g].reshape(rows, LANES)
        qb = (qf * scale).astype(jnp.bfloat16)
        s = lax.dot_general(qf, kc_ref[0], (((1,), (1,)), ((), ())),
                            precision=lax.Precision.HIGHEST, preferred_element_type=jnp.float32)
        s = s * scale + bcmp_ref[g].reshape(rows, nb)
        cmask = jnp.concatenate([tq >= nidx * NSA_BLOCK + NSA_BLOCK - 1] * NSA_REP, axis=0)
        cmaskf = cmask.astype(jnp.float32)
        s = jnp.where(cmask, s, NEG)
        p = jnp.exp(s - jnp.max(s, axis=-1, keepdims=True)) * cmaskf
        pr = p / jnp.maximum(jnp.sum(p, axis=-1, keepdims=True), 1e-30)
        o_cmp = jnp.dot(pr.astype(jnp.bfloat16), vc_ref[0].astype(jnp.bfloat16), preferred_element_type=jnp.float32)
        imp = pr[0:QT]
        for r in range(1, NSA_REP):
            imp = imp + pr[r * QT:(r + 1) * QT]
        cur = tq // NSA_BLOCK
        forced = (nidx == cur) | (nidx == 0)
        score = jnp.where(nidx <= cur, jnp.where(forced, -NEG, imp), NEG)
        cnt = jnp.zeros((QT, nb), jnp.float32)
        for j in range(nb):
            col = score[:, j:j + 1]
            beats = (col > score) | ((col == score) & (nidx > j))
            cnt = cnt + beats.astype(jnp.float32)
        msel = (cnt < float(min(NSA_TOPN, nb))).astype(jnp.bfloat16)
        mexp = jnp.dot(msel, expand, preferred_element_type=jnp.float32)
        for kk in range(n_kt):
            mexp_ref[g, kk] = mexp[:, kk * QT:(kk + 1) * QT]
        o_sel = flash_sel(g, qb)
        o_win = window(g, qb)
        outs.append((o_cmp, o_sel, o_win))

    for r in range(NSA_REP):
        sl = slice(r * QT, (r + 1) * QT)
        ocmp_ref[0, r] = jnp.where(lane_lo, outs[0][0][sl], outs[1][0][sl])
        osel_ref[0, r] = jnp.where(lane_lo, outs[0][1][sl], outs[1][1][sl])
        owin_ref[0, r] = jnp.where(lane_lo, outs[0][2][sl], outs[1][2][sl])


def _bucket_thresholds():
    b = _rel_bucket(jnp.arange(MAX_DISTANCE))
    return jnp.sum(b[None, :] < jnp.arange(1, N_BUCKETS)[:, None], axis=1)


def _bias_of_dist(bt, th, dist):
    ex = (slice(None), slice(None)) + (None,) * dist.ndim
    d = jnp.maximum(dist, 0)[None, None]
    val = jnp.broadcast_to(bt[0][ex], (NSA_GROUPS, NSA_REP) + dist.shape)
    for kk in range(1, N_BUCKETS):
        val = jnp.where(d >= th[kk - 1], bt[kk][ex], val)
    return val


def _nsa_prompt_tables(rel_bias, T):
    nb = T // NSA_BLOCK
    bt = rel_bias.astype(jnp.float32).reshape(N_BUCKETS, NSA_GROUPS, NSA_REP)
    th = _bucket_thresholds()
    d = (jnp.arange(N_DELTA)[:, None, None] * QT + jnp.arange(QT)[None, :, None] - jnp.arange(QT)[None, None, :])
    tiles = _bias_of_dist(bt, th, d)
    dc = jnp.arange(T)[:, None] - (jnp.arange(nb)[None, :] * NSA_BLOCK + NSA_BLOCK - 1)
    bcmp = _bias_of_dist(bt, th, dc)
    return tiles, bcmp


def _group_padded_queries(qt):
    z = jnp.zeros_like(qt[:, 0])
    return jnp.stack([jnp.concatenate([qt[:, 0], z], axis=-1), jnp.concatenate([z, qt[:, 1]], axis=-1)], axis=1)


def _nsa_prompt(q_n, k_cmp, v_cmp, k_sel, v_sel, k_win, v_win, kc_gain, tables):
    B, T = q_n.shape[:2]
    assert T % SEL_KT == 0 and SEL_KT % QT == 0 and QT == 2 * NSA_BLOCK and T >= WINDOW + QT
    nb = T // NSA_BLOCK
    tiles, bcmp = tables
    kc, vc = _block_summaries(k_cmp, v_cmp, kc_gain)
    qt = q_n.transpose(0, 2, 3, 1, 4).reshape(B, NSA_GROUPS, NSA_REP * T, NSA_DIM)
    q_pad = _group_padded_queries(qt).reshape(B, NSA_GROUPS, NSA_REP, T, LANES)
    bf = lambda a: a.astype(jnp.bfloat16)
    kv_spec = pl.BlockSpec((1, T, LANES), lambda b, i: (b, 0, 0))
    sm_spec = pl.BlockSpec((1, nb, LANES), lambda b, i: (b, 0, 0))
    o_spec = pl.BlockSpec((1, NSA_REP, QT, LANES), lambda b, i: (b, 0, i, 0))
    o_shape = jax.ShapeDtypeStruct((B, NSA_REP, T, LANES), jnp.float32)
    outs = pl.pallas_call(
        _nsa_prompt_kernel,
        grid=(B, T // QT),
        in_specs=[pl.BlockSpec((1, NSA_GROUPS, NSA_REP, QT, LANES), lambda b, i: (b, 0, 0, i, 0)),
                  sm_spec, sm_spec,
                  pl.BlockSpec((NSA_GROUPS, NSA_REP, QT, nb), lambda b, i: (0, 0, i, 0)),
                  kv_spec, kv_spec, kv_spec, kv_spec,
                  pl.BlockSpec((NSA_GROUPS, NSA_REP, N_DELTA, QT, QT), lambda b, i: (0, 0, 0, 0, 0))],
        out_specs=[o_spec, o_spec, o_spec],
        out_shape=[o_shape, o_shape, o_shape],
        scratch_shapes=[pltpu.VMEM((NSA_GROUPS, T // QT, QT, QT), jnp.float32),
                        pltpu.VMEM((NSA_REP * QT, 1), jnp.float32),
                        pltpu.VMEM((NSA_REP * QT, 1), jnp.float32),
                        pltpu.VMEM((NSA_REP * QT, LANES), jnp.float32)],
        compiler_params=pltpu.CompilerParams(
            dimension_semantics=("parallel", "arbitrary"), vmem_limit_bytes=VMEM_LIMIT_BYTES),
        name="nsa_prompt_attention",
    )(q_pad, kc, vc, bcmp, bf(k_sel), bf(v_sel), bf(k_win), bf(v_win), tiles)
    back = lambda o: o.reshape(B, NSA_REP, T, NSA_GROUPS, NSA_DIM).transpose(0, 2, 3, 1, 4)
    return tuple(back(o) for o in outs)


def _mla_prompt_kernel(q_ref, k_ref, v_ref, o_ref, m_ref, l_ref, acc_ref):
    qi = pl.program_id(2)
    tq = q_ref.shape[2]
    q = q_ref[0, 0]
    m_ref[...] = jnp.full(m_ref.shape, NEG, jnp.float32)
    l_ref[...] = jnp.zeros(l_ref.shape, jnp.float32)
    acc_ref[...] = jnp.zeros(acc_ref.shape, jnp.float32)

    def step(kt, diagonal):
        k0 = pl.multiple_of(kt * tq, tq)
        k = k_ref[0, 0, pl.ds(k0, tq), :]
        v = v_ref[0, 0, pl.ds(k0, tq), :]
        s = lax.dot_general(q, k, (((1,), (1,)), ((), ())), preferred_element_type=jnp.float32)
        if diagonal:
            ii = lax.broadcasted_iota(jnp.int32, s.shape, 0)
            jj = lax.broadcasted_iota(jnp.int32, s.shape, 1)
            s = jnp.where(jj <= ii, s, NEG)
        m_old = m_ref[...]
        m_new = jnp.maximum(m_old, jnp.max(s, axis=-1, keepdims=True))
        p = jnp.exp(s - m_new)
        alpha = jnp.exp(m_old - m_new)
        l_ref[...] = alpha * l_ref[...] + jnp.sum(p, axis=-1, keepdims=True)
        m_ref[...] = m_new
        acc_ref[...] = alpha * acc_ref[...] + jnp.dot(p.astype(jnp.bfloat16), v, preferred_element_type=jnp.float32)

    def body(kt, carry):
        step(kt, False)
        return carry

    lax.fori_loop(0, qi, body, 0)
    step(qi, True)
    o_ref[0, 0] = acc_ref[...] / l_ref[...]


def _mla_prompt(q, k, v):
    B, T, H, _ = q.shape
    assert T % MLA_TQ == 0
    scale = MLA_QK ** -0.5
    qt = (q * scale).astype(jnp.bfloat16).transpose(0, 2, 1, 3)
    kt = k.astype(jnp.bfloat16).transpose(0, 2, 1, 3)
    vt = v.astype(jnp.bfloat16).transpose(0, 2, 1, 3)
    out = pl.pallas_call(
        _mla_prompt_kernel,
        grid=(B, H, T // MLA_TQ),
        in_specs=[pl.BlockSpec((1, 1, MLA_TQ, MLA_QK), lambda b, h, i: (b, h, i, 0)),
                  pl.BlockSpec((1, 1, T, MLA_QK), lambda b, h, i: (b, h, 0, 0)),
                  pl.BlockSpec((1, 1, T, MLA_V), lambda b, h, i: (b, h, 0, 0))],
        out_specs=pl.BlockSpec((1, 1, MLA_TQ, MLA_V), lambda b, h, i: (b, h, i, 0)),
        out_shape=jax.ShapeDtypeStruct((B, H, T, MLA_V), jnp.float32),
        scratch_shapes=[pltpu.VMEM((MLA_TQ, 1), jnp.float32),
                        pltpu.VMEM((MLA_TQ, 1), jnp.float32),
                        pltpu.VMEM((MLA_TQ, MLA_V), jnp.float32)],
        compiler_params=pltpu.CompilerParams(
            dimension_semantics=("parallel", "parallel", "arbitrary"), vmem_limit_bytes=VMEM_LIMIT_BYTES),
        name="mla_prompt_attention",
    )(qt, kt, vt)
    return out.transpose(0, 2, 1, 3)


def _mla_sample_kernel(layer, n_pages, pt_ref, cache_ref, new_ref, qn_ref, qpe_ref, gn_ref, wuk_ref, wuv_ref,
                       tbl_ref, o_ref, buf_ref, sem_ref, m_ref, l_ref, acc_ref):
    b = pl.program_id(0)
    n_seq = pl.num_programs(0)
    cp = MLA_CHUNK_PAGES
    lc = cp * PAGE_SIZE
    n_chunks = n_pages // cp
    scale = MLA_QK ** -0.5

    def page_copy(seq, chunk, p, slot):
        page = pt_ref[seq, chunk * cp + p]
        return pltpu.make_async_copy(cache_ref.at[layer, page], buf_ref.at[slot, :, pl.ds(p * PAGE_SIZE, PAGE_SIZE)],
                                     sem_ref.at[slot])

    def start_chunk(seq, chunk, slot):
        for p in range(cp):
            page_copy(seq, chunk, p, slot).start()

    def wait_chunk(seq, chunk, slot):
        for p in range(cp):
            page_copy(seq, chunk, p, slot).wait()

    @pl.when(b == 0)
    def _():
        start_chunk(0, 0, 0)

    m_ref[...] = jnp.full(m_ref.shape, NEG, jnp.float32)
    l_ref[...] = jnp.zeros(l_ref.shape, jnp.float32)
    acc_ref[...] = jnp.zeros(acc_ref.shape, jnp.float32)

    qn = (qn_ref[0] * gn_ref[...]).astype(jnp.bfloat16)
    a_abs = jnp.dot(qn, wuk_ref[...], preferred_element_type=jnp.float32)
    lhs1 = jnp.concatenate([wuk_ref[...], a_abs.astype(jnp.bfloat16)], axis=0)
    qpe = qpe_ref[0].astype(jnp.bfloat16)
    n_kn = MLA_HEADS * MLA_NOPE

    def attend(rows_t, col0, mask):
        L = rows_t.shape[1]
        lat = rows_t[:KV_LORA].astype(jnp.bfloat16)
        kpe = rows_t[KV_LORA:]
        x = jnp.dot(lhs1, lat, preferred_element_type=jnp.float32)
        ss_pe = jnp.sum(kpe * kpe, axis=0, keepdims=True)
        f = (jnp.concatenate([kpe, kpe], axis=0) * tbl_ref[:, pl.ds(col0, L)]).astype(jnp.bfloat16)
        s_pe = jnp.dot(qpe, f, preferred_element_type=jnp.float32)
        ps, alphas = [], []
        for h in range(MLA_HEADS):
            kn = x[h * MLA_NOPE:(h + 1) * MLA_NOPE]
            ss = jnp.sum(kn * kn, axis=0, keepdims=True) + ss_pe
            r = lax.rsqrt(ss * (1.0 / MLA_QK) + EPS) * scale
            sl = slice(h * DEC_SEQ, (h + 1) * DEC_SEQ)
            s = (x[n_kn + h * DEC_SEQ:n_kn + (h + 1) * DEC_SEQ] + s_pe[sl]) * r
            if mask is not None:
                s = jnp.where(mask, s, NEG)
            m_old = m_ref[sl]
            m_new = jnp.maximum(m_old, jnp.max(s, axis=-1, keepdims=True))
            p = jnp.exp(s - m_new)
            if mask is not None:
                p = p * mask.astype(jnp.float32)
            alpha = jnp.exp(m_old - m_new)
            l_ref[sl] = alpha * l_ref[sl] + jnp.sum(p, axis=-1, keepdims=True)
            m_ref[sl] = m_new
            ps.append(p.astype(jnp.bfloat16))
            alphas.append(alpha)
        pv = lax.dot_general(jnp.concatenate(ps, axis=0), lat, (((1,), (1,)), ((), ())),
                             preferred_element_type=jnp.float32)
        acc_ref[...] = jnp.concatenate(alphas, axis=0) * acc_ref[...] + pv

    def chunk_body(c, carry):
        slot = c % 2

        @pl.when(c + 1 < n_chunks)
        def _():
            start_chunk(b, c + 1, 1 - slot)

        @pl.when((c + 1 == n_chunks) & (b + 1 < n_seq))
        def _():
            start_chunk(b + 1, 0, 1 - slot)

        wait_chunk(b, c, slot)
        attend(buf_ref[slot], pl.multiple_of(c * lc, lc), None)
        return carry

    lax.fori_loop(0, n_chunks, chunk_body, 0)

    tt = lax.broadcasted_iota(jnp.int32, (DEC_SEQ, PAGE_SIZE), 0)
    jj = lax.broadcasted_iota(jnp.int32, (DEC_SEQ, PAGE_SIZE), 1)
    attend(new_ref[0], n_pages * PAGE_SIZE, jj <= tt)

    o_lat = (acc_ref[...] / jnp.maximum(l_ref[...], 1e-30)).astype(jnp.bfloat16)
    for h in range(MLA_HEADS):
        o_ref[0, h * DEC_SEQ:(h + 1) * DEC_SEQ, :] = jnp.dot(o_lat[h * DEC_SEQ:(h + 1) * DEC_SEQ], wuv_ref[h],
                                                             preferred_element_type=jnp.float32)


def _mla_sample(q_m, rows_new, page_table, cache_mla_t, layer, w_ukv, k_gain, cos, sin):
    DB, S = q_m.shape[:2]
    n_pages = page_table.shape[1]
    assert S == DEC_SEQ and n_pages % MLA_CHUNK_PAGES == 0 and (n_pages // MLA_CHUNK_PAGES) % 2 == 0
    n_pos = n_pages * PAGE_SIZE + PAGE_SIZE
    f32 = jnp.float32
    half = MLA_ROPE // 2
    w = w_ukv.reshape(KV_LORA, MLA_HEADS, MLA_NOPE + MLA_V)
    w_uk = w[:, :, :MLA_NOPE].reshape(KV_LORA, MLA_HEADS * MLA_NOPE).T.astype(jnp.bfloat16)
    w_uv = w[:, :, MLA_NOPE:].transpose(1, 0, 2).astype(jnp.bfloat16)
    qt = q_m.transpose(0, 2, 1, 3)
    eye_h = jnp.eye(MLA_HEADS, dtype=f32)
    q_blk = (qt[:, :, :, None, :MLA_NOPE] * eye_h[None, :, None, :, None]).reshape(
        DB, MLA_ROWS_Q, MLA_HEADS * MLA_NOPE)
    q1, q2 = qt[..., MLA_NOPE:MLA_NOPE + half], qt[..., MLA_NOPE + half:]
    q_pe = jnp.concatenate([q1, q2, q2, -q1], axis=-1).reshape(DB, MLA_ROWS_Q, 2 * MLA_ROPE)
    gn = jnp.tile(k_gain[:MLA_NOPE].astype(f32), MLA_HEADS).reshape(1, MLA_HEADS * MLA_NOPE)
    g1, g2 = k_gain[MLA_NOPE:MLA_NOPE + half].astype(f32), k_gain[MLA_NOPE + half:].astype(f32)
    c, s = cos[:n_pos].T, sin[:n_pos].T
    tbl = jnp.concatenate([c * g1[:, None], c * g2[:, None], s * g1[:, None], s * g2[:, None]], axis=0)
    new_pad = jnp.pad(rows_new.transpose(0, 2, 1), ((0, 0), (0, 0), (0, PAGE_SIZE - S)))
    lc = MLA_CHUNK_PAGES * PAGE_SIZE
    const = lambda shape: pl.BlockSpec(shape, lambda b, pt: (0,) * len(shape))
    out = pl.pallas_call(
        functools.partial(_mla_sample_kernel, layer, n_pages),
        grid_spec=pltpu.PrefetchScalarGridSpec(
            num_scalar_prefetch=1,
            grid=(DB,),
            in_specs=[pl.BlockSpec(memory_space=pl.ANY),
                      pl.BlockSpec((1, MLA_CACHE, PAGE_SIZE), lambda b, pt: (b, 0, 0)),
                      pl.BlockSpec((1, MLA_ROWS_Q, MLA_HEADS * MLA_NOPE), lambda b, pt: (b, 0, 0)),
                      pl.BlockSpec((1, MLA_ROWS_Q, 2 * MLA_ROPE), lambda b, pt: (b, 0, 0)),
                      const((1, MLA_HEADS * MLA_NOPE)),
                      const((MLA_HEADS * MLA_NOPE, KV_LORA)),
                      const((MLA_HEADS, KV_LORA, MLA_V)),
                      const((2 * MLA_ROPE, n_pos))],
            out_specs=pl.BlockSpec((1, MLA_ROWS_Q, MLA_V), lambda b, pt: (b, 0, 0)),
            scratch_shapes=[pltpu.VMEM((2, MLA_CACHE, lc), f32),
                            pltpu.SemaphoreType.DMA((2,)),
                            pltpu.VMEM((MLA_ROWS_Q, 1), f32),
                            pltpu.VMEM((MLA_ROWS_Q, 1), f32),
                            pltpu.VMEM((MLA_ROWS_Q, KV_LORA), f32)]),
        out_shape=jax.ShapeDtypeStruct((DB, MLA_ROWS_Q, MLA_V), f32),
        compiler_params=pltpu.CompilerParams(dimension_semantics=("arbitrary",), vmem_limit_bytes=VMEM_LIMIT_BYTES),
        name="mla_sample_attention",
    )(page_table, cache_mla_t, new_pad, q_blk, q_pe, gn, w_uk, w_uv, tbl)
    return out.reshape(DB, MLA_HEADS, S, MLA_V).transpose(0, 2, 1, 3)


def _nsa_sample_kernel(layer, n_pages, pt_ref, ccmp_ref, csel_ref, q_ref, ncmp_ref, nsel_ref, nwin_ref, wbuf_ref,
                       gain_ref, bcmp_ref, bfar_ref, blast_ref, bnew_ref, bwin_ref, expand_ref, pool_ref,
                       ocmp_ref, osel_ref, owin_ref, buf_ref, sem_ref, sum_ref, msel_ref, m_ref, l_ref, acc_ref):
    b = pl.program_id(0)
    n_seq = pl.num_programs(0)
    cp = NSA_CHUNK_PAGES
    lc = cp * PAGE_SIZE
    n_chunks = n_pages // cp
    bpc = lc // NSA_BLOCK
    n_past = n_pages * PAGE_SIZE
    nbp = n_past // NSA_BLOCK
    nbs = sum_ref.shape[0]
    scale = NSA_DIM ** -0.5

    def page_copy(cache_ref, seq, chunk, p, slot):
        page = pt_ref[seq, chunk * cp + p]
        return pltpu.make_async_copy(cache_ref.at[layer, page], buf_ref.at[slot, :, pl.ds(p * PAGE_SIZE, PAGE_SIZE)],
                                     sem_ref.at[slot])

    def start_chunk(cache_ref, seq, chunk, slot):
        for p in range(cp):
            page_copy(cache_ref, seq, chunk, p, slot).start()

    def wait_chunk(cache_ref, seq, chunk, slot):
        for p in range(cp):
            page_copy(cache_ref, seq, chunk, p, slot).wait()

    @pl.when(b == 0)
    def _():
        start_chunk(ccmp_ref, 0, 0, 0)

    def cmp_body(c, carry):
        slot = c % 2

        @pl.when(c + 1 < n_chunks)
        def _():
            start_chunk(ccmp_ref, b, c + 1, 1 - slot)

        @pl.when(c + 1 == n_chunks)
        def _():
            start_chunk(csel_ref, b, 0, 1 - slot)

        wait_chunk(ccmp_ref, b, c, slot)
        x = buf_ref[slot]
        hi = x.astype(jnp.bfloat16)
        lo = (x - hi.astype(jnp.float32)).astype(jnp.bfloat16)
        nt = (((1,), (1,)), ((), ()))
        sum_ref[pl.ds(pl.multiple_of(c * bpc, bpc), bpc), :] = (
            lax.dot_general(pool_ref[...], hi, nt, preferred_element_type=jnp.float32)
            + lax.dot_general(pool_ref[...], lo, nt, preferred_element_type=jnp.float32))
        return carry

    lax.fori_loop(0, n_chunks, cmp_body, 0)
    pad_rows = nbs - nbp
    new_mean = jnp.sum(ncmp_ref[0], axis=0, keepdims=True) * (1.0 / NSA_BLOCK)
    row0 = lax.broadcasted_iota(jnp.int32, (pad_rows, KV_LANES), 0) == 0
    sum_ref[nbp:, :] = jnp.where(row0, jnp.broadcast_to(new_mean, (pad_rows, KV_LANES)), 0.0)

    summ = sum_ref[...]
    km, vm = summ[:, :LANES], summ[:, LANES:]
    lo_half = lax.broadcasted_iota(jnp.int32, km.shape, 1) < NSA_DIM
    sq = km * km
    ss = jnp.where(lo_half, jnp.sum(jnp.where(lo_half, sq, 0.0), axis=-1, keepdims=True),
                   jnp.sum(jnp.where(lo_half, 0.0, sq), axis=-1, keepdims=True))
    kc = km * lax.rsqrt(ss * (1.0 / NSA_DIM) + EPS) * gain_ref[...]
    vcb = vm.astype(jnp.bfloat16)

    tt = lax.broadcasted_iota(jnp.int32, (ROWS_Q, nbs), 0) % DEC_SEQ
    nn = lax.broadcasted_iota(jnp.int32, (ROWS_Q, nbs), 1)
    cmask = (n_past + tt) >= (nn * NSA_BLOCK + NSA_BLOCK - 1)
    cmaskf = cmask.astype(jnp.float32)
    zq = jnp.zeros((ROWS_Q, NSA_DIM), jnp.float32)
    imps = []
    for g in range(NSA_GROUPS):
        qg = q_ref[0, g]
        q_pad = jnp.concatenate([qg, zq] if g == 0 else [zq, qg], axis=1)
        s = lax.dot_general(q_pad, kc, (((1,), (1,)), ((), ())),
                            precision=lax.Precision.HIGHEST, preferred_element_type=jnp.float32)
        s = jnp.where(cmask, s * scale + bcmp_ref[g], NEG)
        p = jnp.exp(s - jnp.max(s, axis=-1, keepdims=True)) * cmaskf
        pr = p / jnp.maximum(jnp.sum(p, axis=-1, keepdims=True), 1e-30)
        o_full = jnp.dot(pr.astype(jnp.bfloat16), vcb, preferred_element_type=jnp.float32)
        ocmp_ref[0, g] = o_full[:, g * NSA_DIM:(g + 1) * NSA_DIM]
        imp = pr[0:DEC_SEQ]
        for r in range(1, NSA_REP):
            imp = imp + pr[r * DEC_SEQ:(r + 1) * DEC_SEQ]
        imps.append(imp)
    imp = jnp.concatenate(imps, axis=0)
    n16 = lax.broadcasted_iota(jnp.int32, imp.shape, 1)
    t16 = lax.broadcasted_iota(jnp.int32, imp.shape, 0) % DEC_SEQ
    cur = (n_past + t16) // NSA_BLOCK
    forced = (n16 == cur) | (n16 == 0)
    score = jnp.where(n16 <= cur, jnp.where(forced, -NEG, imp), NEG)
    cnt = jnp.zeros(imp.shape, jnp.float32)
    for j in range(nbp + 1):
        cj = score[:, j:j + 1]
        beats = (cj > score) | ((cj == score) & (n16 > j))
        cnt = cnt + beats.astype(jnp.float32)
    msel = (cnt < float(NSA_TOPN)).astype(jnp.float32)
    msel_ref[...] = jnp.zeros(msel_ref.shape, jnp.float32)
    for c in range(n_chunks):
        msel_ref[c, :, 0:bpc] = msel[:, c * bpc:(c + 1) * bpc]
    msel_new = msel[:, nbp:nbp + 1]

    def attend(g, kv_t, bias, mask):
        L = kv_t.shape[1]
        k = kv_t[g * NSA_DIM:(g + 1) * NSA_DIM].astype(jnp.bfloat16)
        v = kv_t[NSA_KV + g * NSA_DIM:NSA_KV + (g + 1) * NSA_DIM].astype(jnp.bfloat16)
        s = jnp.dot(qs[g], k, preferred_element_type=jnp.float32)
        if bias is not None:
            s = s + bias
        s = jnp.where(mask[None], s.reshape(NSA_REP, DEC_SEQ, L), NEG).reshape(ROWS_Q, L)
        m_old = m_ref[g]
        m_new = jnp.maximum(m_old, jnp.max(s, axis=-1, keepdims=True))
        p = jnp.exp(s - m_new)
        alpha = jnp.exp(m_old - m_new)
        l_ref[g] = alpha * l_ref[g] + jnp.sum(p, axis=-1, keepdims=True)
        m_ref[g] = m_new
        acc_ref[g] = alpha * acc_ref[g] + lax.dot_general(p.astype(jnp.bfloat16), v, (((1,), (1,)), ((), ())),
                                                          preferred_element_type=jnp.float32)

    def reset():
        m_ref[...] = jnp.full(m_ref.shape, NEG, jnp.float32)
        l_ref[...] = jnp.zeros(l_ref.shape, jnp.float32)
        acc_ref[...] = jnp.zeros(acc_ref.shape, jnp.float32)

    def result(g):
        return acc_ref[g] / l_ref[g]

    qs = [(q_ref[0, g] * scale).astype(jnp.bfloat16) for g in range(NSA_GROUPS)]
    tn = lax.broadcasted_iota(jnp.int32, (DEC_SEQ, PAGE_SIZE), 0)
    jn = lax.broadcasted_iota(jnp.int32, (DEC_SEQ, PAGE_SIZE), 1)
    new_causal = jn <= tn

    reset()

    def sel_chunk(c, slot, last):
        wait_chunk(csel_ref, b, c, slot)
        kv_t = buf_ref[slot]
        mexp = jnp.dot(msel_ref[c].astype(jnp.bfloat16), expand_ref[...], preferred_element_type=jnp.float32)
        for g in range(NSA_GROUPS):
            attend(g, kv_t, blast_ref[g] if last else None, mexp[g * DEC_SEQ:(g + 1) * DEC_SEQ] > 0.5)

    def sel_body(c, carry):
        slot = c % 2
        start_chunk(csel_ref, b, c + 1, 1 - slot)
        sel_chunk(c, slot, False)
        return carry

    lax.fori_loop(0, n_chunks - 1, sel_body, 0)

    @pl.when(b + 1 < n_seq)
    def _():
        start_chunk(ccmp_ref, b + 1, 0, n_chunks % 2)

    sel_chunk(n_chunks - 1, (n_chunks - 1) % 2, True)
    for g in range(NSA_GROUPS):
        attend(g, nsel_ref[0], bnew_ref[g] - bfar_ref[g], new_causal & (msel_new[g * DEC_SEQ:(g + 1) * DEC_SEQ] > 0.5))
        osel_ref[0, g] = result(g)

    reset()
    n_buf = wbuf_ref.shape[3]
    tw = lax.broadcasted_iota(jnp.int32, (DEC_SEQ, n_buf), 0)
    cw = lax.broadcasted_iota(jnp.int32, (DEC_SEQ, n_buf), 1)
    dist_w = n_buf + tw - cw
    wmask = (dist_w >= 0) & (dist_w < WINDOW)
    for g in range(NSA_GROUPS):
        attend(g, wbuf_ref[0, 0], bwin_ref[g], wmask)
        attend(g, nwin_ref[0], bnew_ref[g], new_causal)
        owin_ref[0, g] = result(g)


def _nsa_sample_tables(rel_bias, n_pages, n_buf):
    n_past = n_pages * PAGE_SIZE
    nbp = n_past // NSA_BLOCK
    nbs = -(-(nbp + 1) // LANES) * LANES
    lc = NSA_CHUNK_PAGES * PAGE_SIZE
    bt = rel_bias.astype(jnp.float32).reshape(N_BUCKETS, NSA_GROUPS, NSA_REP)
    th = _bucket_thresholds()
    t = jnp.arange(DEC_SEQ)

    def table(dist):
        return _bias_of_dist(bt, th, dist).reshape(NSA_GROUPS, ROWS_Q, dist.shape[1])

    bcmp = table(n_past + t[:, None] - (jnp.arange(nbs)[None, :] * NSA_BLOCK + NSA_BLOCK - 1))
    bfar = jnp.broadcast_to(bt[N_BUCKETS - 1][:, :, None, None], (NSA_GROUPS, NSA_REP, DEC_SEQ, 1)).reshape(
        NSA_GROUPS, ROWS_Q, 1)
    blast = table(n_past + t[:, None] - (n_past - lc + jnp.arange(lc))[None, :]) - bfar
    bnew = table(t[:, None] - jnp.arange(PAGE_SIZE)[None, :])
    bwin = table(n_buf + t[:, None] - jnp.arange(n_buf)[None, :])
    expand = (jnp.arange(lc)[None, :] // NSA_BLOCK == jnp.arange(LANES)[:, None]).astype(jnp.bfloat16)
    pool = (expand[:lc // NSA_BLOCK].astype(jnp.float32) * (1.0 / NSA_BLOCK)).astype(jnp.bfloat16)
    return bcmp, bfar, blast, bnew, bwin, expand, pool


def _rows_last(a):
    nd = a.ndim
    perm = tuple(range(nd - 4)) + (nd - 3, nd - 2, nd - 1, nd - 4)
    at = a.transpose(perm)
    return at.reshape(at.shape[:-4] + (KV_LANES, at.shape[-1]))


def _nsa_sample(q_n, kc_new, vc_new, ks_new, vs_new, kw_new, vw_new, page_table, cache_cmp_t, cache_sel_t, win_t,
                layer, kc_gain, tables):
    DB, S = q_n.shape[:2]
    n_pages = page_table.shape[1]
    n_buf = win_t.shape[3]
    lc = NSA_CHUNK_PAGES * PAGE_SIZE
    n_chunks = n_pages // NSA_CHUNK_PAGES
    assert S == DEC_SEQ and n_pages % NSA_CHUNK_PAGES == 0 and n_chunks % 2 == 0 and n_buf == WINDOW
    assert lc + 1 >= MAX_DISTANCE and lc // NSA_BLOCK <= LANES
    assert (n_pages * PAGE_SIZE) % NSA_BLOCK == 0 and S <= NSA_BLOCK
    f32 = jnp.float32
    bcmp, bfar, blast, bnew, bwin, expand, pool = tables
    nbs = bcmp.shape[-1]
    qt = q_n.transpose(0, 2, 3, 1, 4).reshape(DB, NSA_GROUPS, ROWS_Q, NSA_DIM)
    new_t = lambda k, v, width: jnp.pad(jnp.concatenate([k, v], axis=-1).transpose(0, 2, 1),
                                        ((0, 0), (0, 0), (0, width - S)))
    new_cmp = jnp.concatenate([kc_new, vc_new], axis=-1)
    new_sel = new_t(ks_new, vs_new, PAGE_SIZE)
    new_win = new_t(kw_new, vw_new, PAGE_SIZE)
    gain = jnp.tile(kc_gain.astype(f32), NSA_GROUPS).reshape(1, LANES)
    const = lambda shape: pl.BlockSpec(shape, lambda b, pt: (0,) * len(shape))
    per_seq = lambda shape: pl.BlockSpec((1,) + shape, lambda b, pt: (b,) + (0,) * len(shape))
    o_shape = jax.ShapeDtypeStruct((DB, NSA_GROUPS, ROWS_Q, NSA_DIM), f32)
    outs = pl.pallas_call(
        functools.partial(_nsa_sample_kernel, layer, n_pages),
        grid_spec=pltpu.PrefetchScalarGridSpec(
            num_scalar_prefetch=1,
            grid=(DB,),
            in_specs=[pl.BlockSpec(memory_space=pl.ANY), pl.BlockSpec(memory_space=pl.ANY),
                      per_seq((NSA_GROUPS, ROWS_Q, NSA_DIM)),
                      per_seq((S, KV_LANES)), per_seq((KV_LANES, PAGE_SIZE)), per_seq((KV_LANES, PAGE_SIZE)),
                      pl.BlockSpec((1, 1, KV_LANES, n_buf), lambda b, pt: (layer, b, 0, 0)),
                      const((1, LANES)),
                      const((NSA_GROUPS, ROWS_Q, nbs)), const((NSA_GROUPS, ROWS_Q, 1)),
                      const((NSA_GROUPS, ROWS_Q, lc)), const((NSA_GROUPS, ROWS_Q, PAGE_SIZE)),
                      const((NSA_GROUPS, ROWS_Q, n_buf)), const((LANES, lc)), const((lc // NSA_BLOCK, lc))],
            out_specs=[per_seq((NSA_GROUPS, ROWS_Q, NSA_DIM))] * 3,
            scratch_shapes=[pltpu.VMEM((2, KV_LANES, lc), f32),
                            pltpu.SemaphoreType.DMA((2,)),
                            pltpu.VMEM((nbs, KV_LANES), f32),
                            pltpu.VMEM((n_chunks, NSA_GROUPS * DEC_SEQ, LANES), f32),
                            pltpu.VMEM((NSA_GROUPS, ROWS_Q, 1), f32),
                            pltpu.VMEM((NSA_GROUPS, ROWS_Q, 1), f32),
                            pltpu.VMEM((NSA_GROUPS, ROWS_Q, NSA_DIM), f32)]),
        out_shape=[o_shape] * 3,
        compiler_params=pltpu.CompilerParams(dimension_semantics=("arbitrary",), vmem_limit_bytes=VMEM_LIMIT_BYTES),
        name="nsa_sample_attention",
    )(page_table, cache_cmp_t, cache_sel_t, qt, new_cmp, new_sel, new_win, win_t, gain,
      bcmp, bfar, blast, bnew, bwin, expand, pool)
    back = lambda o: o.reshape(DB, NSA_GROUPS, NSA_REP, S, NSA_DIM).transpose(0, 3, 1, 2, 4)
    return tuple(back(o) for o in outs)


def _pool_kernel(pos0, halo_ref, u_ref, w_ref, scale_ref, o_ref, ext_ref):
    i = pl.program_id(1)
    tile = u_ref.shape[1]

    @pl.when(i == 0)
    def _():
        ext_ref[0:POOL_HALO] = halo_ref[0]

    @pl.when(i > 0)
    def _():
        ext_ref[0:POOL_HALO] = ext_ref[tile:tile + POOL_HALO]

    ext_ref[POOL_HALO:POOL_HALO + tile] = u_ref[0]
    pos = pos0 + i * tile + lax.broadcasted_iota(jnp.int32, (tile, 1), 0)
    for gi, w in enumerate(POOL_WINDOWS):
        cols = slice(gi * POOL_GROUP, (gi + 1) * POOL_GROUP)
        x = ext_ref[POOL_HALO:POOL_HALO + tile, cols]
        win_sum = x
        for k in range(1, w):
            win_sum = win_sum + ext_ref[POOL_HALO - k:POOL_HALO - k + tile, cols]
        cnt = jnp.minimum(w, pos + 1).astype(jnp.float32)
        d = win_sum / cnt - x
        y = jnp.dot(d.astype(jnp.bfloat16), w_ref[gi], preferred_element_type=jnp.float32)
        o_ref[0, :, cols] = y * scale_ref[:, cols]


def _pool_mix(u, halo, pos0, w_pool, scale, tile):
    N, L, _ = u.shape
    assert L % tile == 0 and tile % 8 == 0 and (tile >= POOL_HALO or L == tile)
    return pl.pallas_call(
        functools.partial(_pool_kernel, pos0),
        grid=(N, L // tile),
        in_specs=[pl.BlockSpec((1, POOL_HALO, POOL_WIDTH), lambda n, i: (n, 0, 0)),
                  pl.BlockSpec((1, tile, POOL_WIDTH), lambda n, i: (n, i, 0)),
                  pl.BlockSpec((len(POOL_WINDOWS), POOL_GROUP, POOL_GROUP), lambda n, i: (0, 0, 0)),
                  pl.BlockSpec((1, POOL_WIDTH), lambda n, i: (0, 0))],
        out_specs=pl.BlockSpec((1, tile, POOL_WIDTH), lambda n, i: (n, i, 0)),
        out_shape=jax.ShapeDtypeStruct((N, L, POOL_WIDTH), jnp.float32),
        scratch_shapes=[pltpu.VMEM((POOL_HALO + tile, POOL_WIDTH), jnp.float32)],
        compiler_params=pltpu.CompilerParams(
            dimension_semantics=("parallel", "arbitrary"), vmem_limit_bytes=VMEM_LIMIT_BYTES),
        name="pool_mix",
    )(halo, u, w_pool.astype(jnp.bfloat16), scale.astype(jnp.float32).reshape(1, POOL_WIDTH))


def _project(x, pos, p):
    z = _mm(_rmsnorm(x, p['attn_norm']), p['w_in'], tn=384)
    (u_pool, q_lat, kv_lat, k_pe, q_nsa, k_cmp, v_cmp, k_sel, v_sel, k_win, v_win, g_logit) = _split_in(z)
    N, L = x.shape[:2]
    grp = lambda a: a.reshape(N, L, NSA_GROUPS, NSA_DIM)
    q_mla = _mla_q(q_lat, p, pos)
    mla_rows = jnp.concatenate([_rmsnorm(kv_lat, p['mla_kv_norm']), k_pe], axis=-1)
    q_n = _rmsnorm(q_nsa.reshape(N, L, NSA_GROUPS, NSA_REP, NSA_DIM), p['nsa_q_gain'])
    k_sel = _rmsnorm(grp(k_sel), p['nsa_k_gain'][1])
    k_win = _rmsnorm(grp(k_win), p['nsa_k_gain'][2])
    gates = jax.nn.sigmoid(g_logit).reshape(N, L, 3, NSA_GROUPS, NSA_REP)
    return u_pool, q_mla, mla_rows, q_n, grp(k_cmp), grp(v_cmp), k_sel, grp(v_sel), k_win, grp(v_win), gates


def _finish(x, pool_o, mla_o, nsa_o, p):
    N, L = x.shape[:2]
    mix = jnp.concatenate([pool_o, mla_o.reshape(N, L, MLA_WIDTH), nsa_o.reshape(N, L, NSA_WIDTH)], axis=-1)
    x2 = _mm_res(mix.reshape(N * L, MIX_WIDTH), p['w_out'], x.reshape(N * L, D_MODEL))
    h = _rmsnorm(x2, p['ffn_norm'])
    a = _mm_gated(h, p['w_gate'], p['w_up'])
    return _mm_res(a, p['w_down'], x2).reshape(N, L, D_MODEL)


def kernel(x_prompt, x_sample, cache_mla, cache_nsa_cmp, cache_nsa_sel, state_nsa_win, state_pool, page_table,
           rel_bias, attn_norm, w_in, pool_w, pool_scale, mla_q_norm, mla_kv_norm, w_uq, w_ukv,
           mla_q_gain, mla_k_gain, nsa_q_gain, nsa_k_gain, w_out, ffn_norm, w_gate, w_up, w_down):
    xp, xs = x_prompt, x_sample
    B, T = xp.shape[:2]
    DB, S = xs.shape[:2]
    n_pages = page_table.shape[1]
    n_buf = state_nsa_win.shape[2]
    n_ctx = state_pool.shape[2]
    pos_p = jnp.arange(T)
    pos_s = PAST_LEN + jnp.arange(S)
    cos, sin = _rope_tables(n_pages * PAGE_SIZE + PAGE_SIZE)
    prompt_tables = _nsa_prompt_tables(rel_bias, T)
    sample_tables = _nsa_sample_tables(rel_bias, n_pages, n_buf)
    cache_mla_t = cache_mla.transpose(0, 1, 3, 2)
    cache_cmp_t = _rows_last(cache_nsa_cmp)
    cache_sel_t = _rows_last(cache_nsa_sel)
    win_state_t = _rows_last(state_nsa_win)
    pool_halo_p = jnp.zeros((B, POOL_HALO, POOL_WIDTH), jnp.float32)
    pool_halo_s = jnp.pad(state_pool, ((0, 0), (0, 0), (POOL_HALO - n_ctx, 0), (0, 0)))
    flat = lambda a: a.reshape(a.shape[:2] + (NSA_KV,))
    nmla_p, ncmp_p, nsel_p, nwin_p, npool_p = [], [], [], [], []
    nmla_s, ncmp_s, nsel_s, nwin_s, npool_s = [], [], [], [], []
    for l in range(DEPTH):
        p = {'attn_norm': attn_norm[l], 'w_in': w_in[l], 'mla_q_norm': mla_q_norm[l],
             'mla_kv_norm': mla_kv_norm[l], 'w_uq': w_uq[l], 'w_ukv': w_ukv[l],
             'mla_q_gain': mla_q_gain[l], 'mla_k_gain': mla_k_gain[l], 'nsa_q_gain': nsa_q_gain[l],
             'nsa_k_gain': nsa_k_gain[l], 'w_out': w_out[l], 'ffn_norm': ffn_norm[l],
             'w_gate': w_gate[l], 'w_up': w_up[l], 'w_down': w_down[l]}

        u, q_m, rows_m, q_n, kc, vc, ks, vs, kw, vw, gates = _project(xp, pos_p, p)
        pool_o = _pool_mix(u, pool_halo_p, 0, pool_w[l], pool_scale[l], POOL_TILE)
        k_m, v_m = _mla_kv(rows_m, p, pos_p)
        mla_o = _mla_prompt(q_m, k_m, v_m)
        o_cmp, o_sel, o_win = _nsa_prompt(q_n, flat(kc), flat(vc), flat(ks), flat(vs), flat(kw), flat(vw),
                                          p['nsa_k_gain'][0], prompt_tables)
        nsa_o = _nsa_combine(gates, o_cmp, o_sel, o_win)
        xp = _finish(xp, pool_o, mla_o, nsa_o, p)
        nmla_p.append(rows_m)
        ncmp_p.append(jnp.stack([kc, vc], axis=2))
        nsel_p.append(jnp.stack([ks, vs], axis=2))
        nwin_p.append(jnp.stack([kw, vw], axis=2)[:, T - min(WINDOW, T):])
        npool_p.append(u[:, T - POOL_STATE:])

        u, q_m, rows_m, q_n, kc, vc, ks, vs, kw, vw, gates = _project(xs, pos_s, p)
        u_ext = jnp.concatenate([state_pool[l], u], axis=1)
        pool_o = _pool_mix(u, pool_halo_s[l], PAST_LEN, pool_w[l], pool_scale[l], S)
        mla_o = _mla_sample(q_m, rows_m, page_table, cache_mla_t, l, p['w_ukv'], p['mla_k_gain'], cos, sin)
        o_cmp, o_sel, o_win = _nsa_sample(q_n, flat(kc), flat(vc), flat(ks), flat(vs), flat(kw), flat(vw),
                                          page_table, cache_cmp_t, cache_sel_t, win_state_t, l,
                                          p['nsa_k_gain'][0], sample_tables)
        nsa_o = _nsa_combine(gates, o_cmp, o_sel, o_win)
        xs = _finish(xs, pool_o, mla_o, nsa_o, p)
        nmla_s.append(rows_m)
        ncmp_s.append(jnp.stack([kc, vc], axis=2))
        nsel_s.append(jnp.stack([ks, vs], axis=2))
        nwin_s.append(jnp.concatenate([state_nsa_win[l], jnp.stack([kw, vw], axis=2)], axis=1)[:, S:S + n_buf])
        npool_s.append(u_ext[:, S:S + n_ctx])

    return (xp, xs,
            jnp.stack(nmla_p), jnp.stack(ncmp_p), jnp.stack(nsel_p), jnp.stack(nwin_p), jnp.stack(npool_p),
            jnp.stack(nmla_s), jnp.stack(ncmp_s), jnp.stack(nsel_s), jnp.stack(nwin_s), jnp.stack(npool_s))
```

```python
import functools
import math

import jax
import jax.numpy as jnp
import numpy as np
from jax import lax
from jax.experimental import pallas as pl
from jax.experimental.pallas import tpu as pltpu

D_MODEL = 2048
BATCH = 2
SEQ = 4096
DEPTH = 2
DEC_BATCH = 128
DEC_SEQ = 8
PAST_LEN = 16384
PAGE_SIZE = 128
POOL_WINDOWS = (2, 4, 8, 16)
POOL_WIDTH = D_MODEL // 4
POOL_GROUP = POOL_WIDTH // len(POOL_WINDOWS)
POOL_STATE = max(POOL_WINDOWS) - 1
MLA_HEADS = 6
MLA_NOPE = 64
MLA_ROPE = 32
MLA_QK = MLA_NOPE + MLA_ROPE
MLA_V = 128
MLA_WIDTH = MLA_HEADS * MLA_V
Q_LORA = 384
KV_LORA = 128
MLA_CACHE = KV_LORA + MLA_ROPE
ROPE_THETA = 10000.0
NSA_HEADS = 12
NSA_DIM = 64
NSA_GROUPS = 2
NSA_REP = NSA_HEADS // NSA_GROUPS
NSA_WIDTH = NSA_HEADS * NSA_DIM
NSA_KV = NSA_GROUPS * NSA_DIM
NSA_BLOCK = 64
NSA_TOPN = 16
WINDOW = 512
MIX_WIDTH = POOL_WIDTH + MLA_WIDTH + NSA_WIDTH
IN_SPLITS = (POOL_WIDTH, Q_LORA, KV_LORA, MLA_ROPE, NSA_WIDTH) + (NSA_KV,) * 6 + (3 * NSA_HEADS,)
IN_COLS = sum(IN_SPLITS)
N_BUCKETS = 32
MAX_DISTANCE = 1024
D_FF = -(-8 * D_MODEL // (3 * 256)) * 256
Q_BLOCK = 128
EPS = 1e-6
NEG = -1e30

LANES = 128
KV_LANES = 2 * NSA_KV
POOL_TILE = 512
POOL_HALO = 16
VMEM_LIMIT_BYTES = 48 * 1024 * 1024
QT = 128
SEL_KT = 512
N_DELTA = 9
MLA_TQ = 512
MLA_CHUNK_PAGES = 64
NSA_CHUNK_PAGES = 64
ROWS_Q = NSA_REP * DEC_SEQ
MLA_ROWS_Q = MLA_HEADS * DEC_SEQ


def _mm_kernel(x_ref, w_ref, o_ref):
    o_ref[...] = jnp.dot(x_ref[...], w_ref[...], preferred_element_type=jnp.float32).astype(o_ref.dtype)


def _pmm(x, w, tm=512, tn=512, out_dtype=jnp.float32):
    M, K = x.shape
    N = w.shape[1]
    tm = min(tm, M)
    assert M % tm == 0
    n_pad = -(-N // tn) * tn
    xb = x.astype(jnp.bfloat16)
    wb = w.astype(jnp.bfloat16)
    if n_pad != N:
        wb = jnp.pad(wb, ((0, 0), (0, n_pad - N)))
    out = pl.pallas_call(
        _mm_kernel,
        grid=(M // tm, n_pad // tn),
        in_specs=[pl.BlockSpec((tm, K), lambda i, j: (i, 0)),
                  pl.BlockSpec((K, tn), lambda i, j: (0, j))],
        out_specs=pl.BlockSpec((tm, tn), lambda i, j: (i, j)),
        out_shape=jax.ShapeDtypeStruct((M, n_pad), out_dtype),
        compiler_params=pltpu.CompilerParams(
            dimension_semantics=("parallel", "parallel"), vmem_limit_bytes=VMEM_LIMIT_BYTES),
        name="dense_mm",
    )(xb, wb)
    return out[:, :N] if n_pad != N else out


def _gated_kernel(x_ref, wg_ref, wu_ref, o_ref):
    g = jnp.dot(x_ref[...], wg_ref[...], preferred_element_type=jnp.float32)
    u = jnp.dot(x_ref[...], wu_ref[...], preferred_element_type=jnp.float32)
    o_ref[...] = (g * jax.nn.sigmoid(g) * u).astype(o_ref.dtype)


def _mm_gated(x, wg, wu, tm=512, tn=512):
    M, K = x.shape
    N = wg.shape[1]
    tm = min(tm, M)
    assert M % tm == 0 and N % tn == 0
    bf = lambda a: a.astype(jnp.bfloat16)
    w_spec = pl.BlockSpec((K, tn), lambda i, j: (0, j))
    return pl.pallas_call(
        _gated_kernel,
        grid=(M // tm, N // tn),
        in_specs=[pl.BlockSpec((tm, K), lambda i, j: (i, 0)), w_spec, w_spec],
        out_specs=pl.BlockSpec((tm, tn), lambda i, j: (i, j)),
        out_shape=jax.ShapeDtypeStruct((M, N), jnp.bfloat16),
        compiler_params=pltpu.CompilerParams(
            dimension_semantics=("parallel", "parallel"), vmem_limit_bytes=VMEM_LIMIT_BYTES),
        name="ffn_gate_up",
    )(bf(x), bf(wg), bf(wu))


def _res_kernel(x_ref, w_ref, r_ref, o_ref):
    o_ref[...] = r_ref[...] + jnp.dot(x_ref[...], w_ref[...], preferred_element_type=jnp.float32)


def _mm_res(x, w, res, tm=512, tn=512):
    M, K = x.shape
    N = w.shape[1]
    tm = min(tm, M)
    assert M % tm == 0 and N % tn == 0
    return pl.pallas_call(
        _res_kernel,
        grid=(M // tm, N // tn),
        in_specs=[pl.BlockSpec((tm, K), lambda i, j: (i, 0)),
                  pl.BlockSpec((K, tn), lambda i, j: (0, j)),
                  pl.BlockSpec((tm, tn), lambda i, j: (i, j))],
        out_specs=pl.BlockSpec((tm, tn), lambda i, j: (i, j)),
        out_shape=jax.ShapeDtypeStruct((M, N), jnp.float32),
        compiler_params=pltpu.CompilerParams(
            dimension_semantics=("parallel", "parallel"), vmem_limit_bytes=VMEM_LIMIT_BYTES),
        name="dense_mm_residual",
    )(x.astype(jnp.bfloat16), w.astype(jnp.bfloat16), res)


def _mm(x, w, **kw):
    lead = x.shape[:-1]
    return _pmm(x.reshape(-1, x.shape[-1]), w, **kw).reshape(lead + (w.shape[1],))


def _rmsnorm(x, g):
    xf = x.astype(jnp.float32)
    y = xf * lax.rsqrt(jnp.mean(xf * xf, axis=-1, keepdims=True) + EPS)
    return (y * g.astype(jnp.float32)).astype(x.dtype)


def _rope_tables(n_pos):
    half = MLA_ROPE // 2
    inv = ROPE_THETA ** (-jnp.arange(half, dtype=jnp.float32) / half)
    ang = jnp.arange(n_pos).astype(jnp.float32)[:, None] * inv
    return jnp.cos(ang), jnp.sin(ang)


def _rope(x, pos):
    half = x.shape[-1] // 2
    inv = ROPE_THETA ** (-jnp.arange(half, dtype=jnp.float32) / half)
    ang = pos.astype(jnp.float32)[:, None] * inv
    cos, sin = jnp.cos(ang)[:, None, :], jnp.sin(ang)[:, None, :]
    xf = x.astype(jnp.float32)
    x1, x2 = xf[..., :half], xf[..., half:]
    return jnp.concatenate([x1 * cos - x2 * sin, x1 * sin + x2 * cos], axis=-1).astype(x.dtype)


def _rel_bucket(dist):
    n = jnp.maximum(dist, 0)
    exact = N_BUCKETS // 2
    nf = jnp.maximum(n, exact).astype(jnp.float32)
    large = exact + (jnp.log(nf / exact) / math.log(MAX_DISTANCE / exact) * (N_BUCKETS - exact)).astype(jnp.int32)
    return jnp.where(n < exact, n, jnp.minimum(large, N_BUCKETS - 1))


def _split_in(z):
    cuts = [int(c) for c in np.cumsum(IN_SPLITS)[:-1]]
    return jnp.split(z, cuts, axis=-1)


def _mla_q(q_lat, p, pos):
    q = _mm(_rmsnorm(q_lat, p['mla_q_norm']), p['w_uq'])
    q = _rmsnorm(q.reshape(q.shape[:-1] + (MLA_HEADS, MLA_QK)), p['mla_q_gain'])
    return jnp.concatenate([q[..., :MLA_NOPE], _rope(q[..., MLA_NOPE:], pos)], axis=-1)


def _mla_kv(rows, p, pos):
    kv = (rows[..., :KV_LORA] @ p['w_ukv']).reshape(rows.shape[:-1] + (MLA_HEADS, MLA_NOPE + MLA_V))
    k_pe = jnp.broadcast_to(rows[..., None, KV_LORA:], kv.shape[:-1] + (MLA_ROPE,))
    k = _rmsnorm(jnp.concatenate([kv[..., :MLA_NOPE], k_pe], axis=-1), p['mla_k_gain'])
    k = jnp.concatenate([k[..., :MLA_NOPE], _rope(k[..., MLA_NOPE:], pos)], axis=-1)
    return k, kv[..., MLA_NOPE:]


def _nsa_combine(gates, o_cmp, o_sel, o_win):
    g = gates[..., None]
    return g[:, :, 0] * o_cmp + g[:, :, 1] * o_sel + g[:, :, 2] * o_win


def _blocksum_kernel(k_ref, v_ref, gain_ref, kc_ref, vc_ref):
    nb = kc_ref.shape[1]
    k = k_ref[0].reshape(nb, NSA_BLOCK, LANES)
    v = v_ref[0].reshape(nb, NSA_BLOCK, LANES)
    km = jnp.sum(k, axis=1) * (1.0 / NSA_BLOCK)
    vm = jnp.sum(v, axis=1) * (1.0 / NSA_BLOCK)
    lo = lax.broadcasted_iota(jnp.int32, km.shape, 1) < NSA_DIM
    sq = km * km
    ss_lo = jnp.sum(jnp.where(lo, sq, 0.0), axis=-1, keepdims=True)
    ss_hi = jnp.sum(jnp.where(lo, 0.0, sq), axis=-1, keepdims=True)
    ss = jnp.where(lo, ss_lo, ss_hi)
    kc_ref[0] = km * lax.rsqrt(ss * (1.0 / NSA_DIM) + EPS) * gain_ref[...]
    vc_ref[0] = vm


def _block_summaries(k_cmp, v_cmp, kc_gain):
    B, T, _ = k_cmp.shape
    nb = T // NSA_BLOCK
    gain = jnp.tile(kc_gain.astype(jnp.float32), NSA_GROUPS).reshape(1, LANES)
    spec = pl.BlockSpec((1, T, LANES), lambda b: (b, 0, 0))
    ospec = pl.BlockSpec((1, nb, LANES), lambda b: (b, 0, 0))
    return pl.pallas_call(
        _blocksum_kernel,
        grid=(B,),
        in_specs=[spec, spec, pl.BlockSpec((1, LANES), lambda b: (0, 0))],
        out_specs=[ospec, ospec],
        out_shape=[jax.ShapeDtypeStruct((B, nb, LANES), jnp.float32)] * 2,
        compiler_params=pltpu.CompilerParams(dimension_semantics=("parallel",), vmem_limit_bytes=VMEM_LIMIT_BYTES),
        name="nsa_block_summaries",
    )(k_cmp, v_cmp, gain)


def _nsa_prompt_kernel(q_ref, kc_ref, vc_ref, bcmp_ref, ks_ref, vs_ref, kw_ref, vw_ref, btile_ref,
                       ocmp_ref, osel_ref, owin_ref, mexp_ref, m_ref, l_ref, acc_ref):
    qi = pl.program_id(1)
    T = ks_ref.shape[1]
    nb = kc_ref.shape[1]
    n_kt = T // QT
    scale = NSA_DIM ** -0.5
    rows = NSA_REP * QT

    tq = lax.broadcasted_iota(jnp.int32, (QT, nb), 0) + qi * QT
    nidx = lax.broadcasted_iota(jnp.int32, (QT, nb), 1)
    ii = lax.broadcasted_iota(jnp.int32, (QT, QT), 0)
    jj = lax.broadcasted_iota(jnp.int32, (QT, QT), 1)
    expand = (lax.broadcasted_iota(jnp.int32, (nb, T), 1) // NSA_BLOCK
              == lax.broadcasted_iota(jnp.int32, (nb, T), 0)).astype(jnp.bfloat16)
    lane_lo = lax.broadcasted_iota(jnp.int32, (QT, LANES), 1) < NSA_DIM

    def block_mask_bias(g, kb, windowed):
        delta = qi - kb
        dist = delta * QT + ii - jj
        if windowed:
            mask = (dist >= 0) & (dist < WINDOW)
        else:
            mask = (dist >= 0) & (mexp_ref[g, kb] > 0.5)
        dcl = jnp.clip(delta, 0, N_DELTA - 1)
        return mask, [btile_ref[g, r, dcl] for r in range(NSA_REP)]

    def tile_inputs(g, kb0, n_blk, windowed):
        parts = [block_mask_bias(g, kb0 + j, windowed) for j in range(n_blk)]
        mask = jnp.concatenate([p[0] for p in parts], axis=1)
        biases = [jnp.concatenate([p[1][r] for p in parts], axis=1) for r in range(NSA_REP)]
        return mask, biases

    def flash_sel(g, qb):
        m_ref[...] = jnp.full(m_ref.shape, NEG, jnp.float32)
        l_ref[...] = jnp.zeros(l_ref.shape, jnp.float32)
        acc_ref[...] = jnp.zeros(acc_ref.shape, jnp.float32)
        n_blk = SEL_KT // QT

        def body(kt, carry):
            k0 = pl.multiple_of(kt * SEL_KT, SEL_KT)
            k = ks_ref[0, pl.ds(k0, SEL_KT), :]
            v = vs_ref[0, pl.ds(k0, SEL_KT), :]
            s = lax.dot_general(qb, k, (((1,), (1,)), ((), ())), preferred_element_type=jnp.float32)
            mask, biases = tile_inputs(g, kt * n_blk, n_blk, False)
            ps, alphas = [], []
            for r in range(NSA_REP):
                sl = slice(r * QT, (r + 1) * QT)
                sr = jnp.where(mask, s[sl] + biases[r], NEG)
                m_old = m_ref[sl]
                m_new = jnp.maximum(m_old, jnp.max(sr, axis=-1, keepdims=True))
                p = jnp.exp(sr - m_new)
                alpha = jnp.exp(m_old - m_new)
                l_ref[sl] = alpha * l_ref[sl] + jnp.sum(p, axis=-1, keepdims=True)
                m_ref[sl] = m_new
                ps.append(p.astype(jnp.bfloat16))
                alphas.append(alpha)
            pv = jnp.dot(jnp.concatenate(ps, axis=0), v, preferred_element_type=jnp.float32)
            acc_ref[...] = jnp.concatenate(alphas, axis=0) * acc_ref[...] + pv
            return carry

        lax.fori_loop(0, qi // n_blk + 1, body, 0)
        return acc_ref[...] / jnp.maximum(l_ref[...], 1e-30)

    def window(g, qb):
        n_blk = WINDOW // QT + 1
        kb0 = jnp.maximum(qi - WINDOW // QT, 0)
        k0 = pl.multiple_of(kb0 * QT, QT)
        k = kw_ref[0, pl.ds(k0, n_blk * QT), :]
        v = vw_ref[0, pl.ds(k0, n_blk * QT), :]
        s = lax.dot_general(qb, k, (((1,), (1,)), ((), ())), preferred_element_type=jnp.float32)
        mask, biases = tile_inputs(g, kb0, n_blk, True)
        ps, ls = [], []
        for r in range(NSA_REP):
            sr = jnp.where(mask, s[r * QT:(r + 1) * QT] + biases[r], NEG)
            p = jnp.exp(sr - jnp.max(sr, axis=-1, keepdims=True))
            ls.append(jnp.sum(p, axis=-1, keepdims=True))
            ps.append(p.astype(jnp.bfloat16))
        pv = jnp.dot(jnp.concatenate(ps, axis=0), v, preferred_element_type=jnp.float32)
        return pv / jnp.maximum(jnp.concatenate(ls, axis=0), 1e-30)

    outs = []
    for g in range(NSA_GROUPS):
        qf = q_ref[0, g].reshape(rows, LANES)
        qb = (qf * scale).astype(jnp.bfloat16)
        s = lax.dot_general(qf, kc_ref[0], (((1,), (1,)), ((), ())),
                            precision=lax.Precision.HIGHEST, preferred_element_type=jnp.float32)
        s = s * scale + bcmp_ref[g].reshape(rows, nb)
        cmask = jnp.concatenate([tq >= nidx * NSA_BLOCK + NSA_BLOCK - 1] * NSA_REP, axis=0)
        cmaskf = cmask.astype(jnp.float32)
        s = jnp.where(cmask, s, NEG)
        p = jnp.exp(s - jnp.max(s, axis=-1, keepdims=True)) * cmaskf
        pr = p / jnp.maximum(jnp.sum(p, axis=-1, keepdims=True), 1e-30)
        o_cmp = jnp.dot(pr.astype(jnp.bfloat16), vc_ref[0].astype(jnp.bfloat16), preferred_element_type=jnp.float32)
        imp = pr[0:QT]
        for r in range(1, NSA_REP):
            imp = imp + pr[r * QT:(r + 1) * QT]
        cur = tq // NSA_BLOCK
        forced = (nidx == cur) | (nidx == 0)
        score = jnp.where(nidx <= cur, jnp.where(forced, -NEG, imp), NEG)
        cnt = jnp.zeros((QT, nb), jnp.float32)
        for j in range(nb):
            col = score[:, j:j + 1]
            beats = (col > score) | ((col == score) & (nidx > j))
            cnt = cnt + beats.astype(jnp.float32)
        msel = (cnt < float(min(NSA_TOPN, nb))).astype(jnp.bfloat16)
        mexp = jnp.dot(msel, expand, preferred_element_type=jnp.float32)
        for kk in range(n_kt):
            mexp_ref[g, kk] = mexp[:, kk * QT:(kk + 1) * QT]
        o_sel = flash_sel(g, qb)
        o_win = window(g, qb)
        outs.append((o_cmp, o_sel, o_win))

    for r in range(NSA_REP):
        sl = slice(r * QT, (r + 1) * QT)
        ocmp_ref[0, r] = jnp.where(lane_lo, outs[0][0][sl], outs[1][0][sl])
        osel_ref[0, r] = jnp.where(lane_lo, outs[0][1][sl], outs[1][1][sl])
        owin_ref[0, r] = jnp.where(lane_lo, outs[0][2][sl], outs[1][2][sl])


def _bucket_thresholds():
    b = _rel_bucket(jnp.arange(MAX_DISTANCE))
    return jnp.sum(b[None, :] < jnp.arange(1, N_BUCKETS)[:, None], axis=1)


def _bias_of_dist(bt, th, dist):
    ex = (slice(None), slice(None)) + (None,) * dist.ndim
    d = jnp.maximum(dist, 0)[None, None]
    val = jnp.broadcast_to(bt[0][ex], (NSA_GROUPS, NSA_REP) + dist.shape)
    for kk in range(1, N_BUCKETS):
        val = jnp.where(d >= th[kk - 1], bt[kk][ex], val)
    return val


def _nsa_prompt_tables(rel_bias, T):
    nb = T // NSA_BLOCK
    bt = rel_bias.astype(jnp.float32).reshape(N_BUCKETS, NSA_GROUPS, NSA_REP)
    th = _bucket_thresholds()
    d = (jnp.arange(N_DELTA)[:, None, None] * QT + jnp.arange(QT)[None, :, None] - jnp.arange(QT)[None, None, :])
    tiles = _bias_of_dist(bt, th, d)
    dc = jnp.arange(T)[:, None] - (jnp.arange(nb)[None, :] * NSA_BLOCK + NSA_BLOCK - 1)
    bcmp = _bias_of_dist(bt, th, dc)
    return tiles, bcmp


def _group_padded_queries(qt):
    z = jnp.zeros_like(qt[:, 0])
    return jnp.stack([jnp.concatenate([qt[:, 0], z], axis=-1), jnp.concatenate([z, qt[:, 1]], axis=-1)], axis=1)


def _nsa_prompt(q_n, k_cmp, v_cmp, k_sel, v_sel, k_win, v_win, kc_gain, tables):
    B, T = q_n.shape[:2]
    assert T % SEL_KT == 0 and SEL_KT % QT == 0 and QT == 2 * NSA_BLOCK and T >= WINDOW + QT
    nb = T // NSA_BLOCK
    tiles, bcmp = tables
    kc, vc = _block_summaries(k_cmp, v_cmp, kc_gain)
    qt = q_n.transpose(0, 2, 3, 1, 4).reshape(B, NSA_GROUPS, NSA_REP * T, NSA_DIM)
    q_pad = _group_padded_queries(qt).reshape(B, NSA_GROUPS, NSA_REP, T, LANES)
    bf = lambda a: a.astype(jnp.bfloat16)
    kv_spec = pl.BlockSpec((1, T, LANES), lambda b, i: (b, 0, 0))
    sm_spec = pl.BlockSpec((1, nb, LANES), lambda b, i: (b, 0, 0))
    o_spec = pl.BlockSpec((1, NSA_REP, QT, LANES), lambda b, i: (b, 0, i, 0))
    o_shape = jax.ShapeDtypeStruct((B, NSA_REP, T, LANES), jnp.float32)
    outs = pl.pallas_call(
        _nsa_prompt_kernel,
        grid=(B, T // QT),
        in_specs=[pl.BlockSpec((1, NSA_GROUPS, NSA_REP, QT, LANES), lambda b, i: (b, 0, 0, i, 0)),
                  sm_spec, sm_spec,
                  pl.BlockSpec((NSA_GROUPS, NSA_REP, QT, nb), lambda b, i: (0, 0, i, 0)),
                  kv_spec, kv_spec, kv_spec, kv_spec,
                  pl.BlockSpec((NSA_GROUPS, NSA_REP, N_DELTA, QT, QT), lambda b, i: (0, 0, 0, 0, 0))],
        out_specs=[o_spec, o_spec, o_spec],
        out_shape=[o_shape, o_shape, o_shape],
        scratch_shapes=[pltpu.VMEM((NSA_GROUPS, T // QT, QT, QT), jnp.float32),
                        pltpu.VMEM((NSA_REP * QT, 1), jnp.float32),
                        pltpu.VMEM((NSA_REP * QT, 1), jnp.float32),
                        pltpu.VMEM((NSA_REP * QT, LANES), jnp.float32)],
        compiler_params=pltpu.CompilerParams(
            dimension_semantics=("parallel", "arbitrary"), vmem_limit_bytes=VMEM_LIMIT_BYTES),
        name="nsa_prompt_attention",
    )(q_pad, kc, vc, bcmp, bf(k_sel), bf(v_sel), bf(k_win), bf(v_win), tiles)
    back = lambda o: o.reshape(B, NSA_REP, T, NSA_GROUPS, NSA_DIM).transpose(0, 2, 3, 1, 4)
    return tuple(back(o) for o in outs)


def _mla_prompt_kernel(q_ref, k_ref, v_ref, o_ref, m_ref, l_ref, acc_ref):
    qi = pl.program_id(2)
    tq = q_ref.shape[2]
    q = q_ref[0, 0]
    m_ref[...] = jnp.full(m_ref.shape, NEG, jnp.float32)
    l_ref[...] = jnp.zeros(l_ref.shape, jnp.float32)
    acc_ref[...] = jnp.zeros(acc_ref.shape, jnp.float32)

    def step(kt, diagonal):
        k0 = pl.multiple_of(kt * tq, tq)
        k = k_ref[0, 0, pl.ds(k0, tq), :]
        v = v_ref[0, 0, pl.ds(k0, tq), :]
        s = lax.dot_general(q, k, (((1,), (1,)), ((), ())), preferred_element_type=jnp.float32)
        if diagonal:
            ii = lax.broadcasted_iota(jnp.int32, s.shape, 0)
            jj = lax.broadcasted_iota(jnp.int32, s.shape, 1)
            s = jnp.where(jj <= ii, s, NEG)
        m_old = m_ref[...]
        m_new = jnp.maximum(m_old, jnp.max(s, axis=-1, keepdims=True))
        p = jnp.exp(s - m_new)
        alpha = jnp.exp(m_old - m_new)
        l_ref[...] = alpha * l_ref[...] + jnp.sum(p, axis=-1, keepdims=True)
        m_ref[...] = m_new
        acc_ref[...] = alpha * acc_ref[...] + jnp.dot(p.astype(jnp.bfloat16), v, preferred_element_type=jnp.float32)

    def body(kt, carry):
        step(kt, False)
        return carry

    lax.fori_loop(0, qi, body, 0)
    step(qi, True)
    o_ref[0, 0] = acc_ref[...] / l_ref[...]


def _mla_prompt(q, k, v):
    B, T, H, _ = q.shape
    assert T % MLA_TQ == 0
    scale = MLA_QK ** -0.5
    qt = (q * scale).astype(jnp.bfloat16).transpose(0, 2, 1, 3)
    kt = k.astype(jnp.bfloat16).transpose(0, 2, 1, 3)
    vt = v.astype(jnp.bfloat16).transpose(0, 2, 1, 3)
    out = pl.pallas_call(
        _mla_prompt_kernel,
        grid=(B, H, T // MLA_TQ),
        in_specs=[pl.BlockSpec((1, 1, MLA_TQ, MLA_QK), lambda b, h, i: (b, h, i, 0)),
                  pl.BlockSpec((1, 1, T, MLA_QK), lambda b, h, i: (b, h, 0, 0)),
                  pl.BlockSpec((1, 1, T, MLA_V), lambda b, h, i: (b, h, 0, 0))],
        out_specs=pl.BlockSpec((1, 1, MLA_TQ, MLA_V), lambda b, h, i: (b, h, i, 0)),
        out_shape=jax.ShapeDtypeStruct((B, H, T, MLA_V), jnp.float32),
        scratch_shapes=[pltpu.VMEM((MLA_TQ, 1), jnp.float32),
                        pltpu.VMEM((MLA_TQ, 1), jnp.float32),
                        pltpu.VMEM((MLA_TQ, MLA_V), jnp.float32)],
        compiler_params=pltpu.CompilerParams(
            dimension_semantics=("parallel", "parallel", "arbitrary"), vmem_limit_bytes=VMEM_LIMIT_BYTES),
        name="mla_prompt_attention",
    )(qt, kt, vt)
    return out.transpose(0, 2, 1, 3)


def _mla_sample_kernel(layer, n_pages, pt_ref, cache_ref, new_ref, qn_ref, qpe_ref, gn_ref, wuk_ref, wuv_ref,
                       tbl_ref, o_ref, buf_ref, sem_ref, m_ref, l_ref, acc_ref):
    b = pl.program_id(0)
    n_seq = pl.num_programs(0)
    cp = MLA_CHUNK_PAGES
    lc = cp * PAGE_SIZE
    n_chunks = n_pages // cp
    scale = MLA_QK ** -0.5

    def page_copy(seq, chunk, p, slot):
        page = pt_ref[seq, chunk * cp + p]
        return pltpu.make_async_copy(cache_ref.at[layer, page], buf_ref.at[slot, :, pl.ds(p * PAGE_SIZE, PAGE_SIZE)],
                                     sem_ref.at[slot])

    def start_chunk(seq, chunk, slot):
        for p in range(cp):
            page_copy(seq, chunk, p, slot).start()

    def wait_chunk(seq, chunk, slot):
        for p in range(cp):
            page_copy(seq, chunk, p, slot).wait()

    @pl.when(b == 0)
    def _():
        start_chunk(0, 0, 0)

    m_ref[...] = jnp.full(m_ref.shape, NEG, jnp.float32)
    l_ref[...] = jnp.zeros(l_ref.shape, jnp.float32)
    acc_ref[...] = jnp.zeros(acc_ref.shape, jnp.float32)

    qn = (qn_ref[0] * gn_ref[...]).astype(jnp.bfloat16)
    a_abs = jnp.dot(qn, wuk_ref[...], preferred_element_type=jnp.float32)
    lhs1 = jnp.concatenate([wuk_ref[...], a_abs.astype(jnp.bfloat16)], axis=0)
    qpe = qpe_ref[0].astype(jnp.bfloat16)
    n_kn = MLA_HEADS * MLA_NOPE

    def attend(rows_t, col0, mask):
        L = rows_t.shape[1]
        lat = rows_t[:KV_LORA].astype(jnp.bfloat16)
        kpe = rows_t[KV_LORA:]
        x = jnp.dot(lhs1, lat, preferred_element_type=jnp.float32)
        ss_pe = jnp.sum(kpe * kpe, axis=0, keepdims=True)
        f = (jnp.concatenate([kpe, kpe], axis=0) * tbl_ref[:, pl.ds(col0, L)]).astype(jnp.bfloat16)
        s_pe = jnp.dot(qpe, f, preferred_element_type=jnp.float32)
        ps, alphas = [], []
        for h in range(MLA_HEADS):
            kn = x[h * MLA_NOPE:(h + 1) * MLA_NOPE]
            ss = jnp.sum(kn * kn, axis=0, keepdims=True) + ss_pe
            r = lax.rsqrt(ss * (1.0 / MLA_QK) + EPS) * scale
            sl = slice(h * DEC_SEQ, (h + 1) * DEC_SEQ)
            s = (x[n_kn + h * DEC_SEQ:n_kn + (h + 1) * DEC_SEQ] + s_pe[sl]) * r
            if mask is not None:
                s = jnp.where(mask, s, NEG)
            m_old = m_ref[sl]
            m_new = jnp.maximum(m_old, jnp.max(s, axis=-1, keepdims=True))
            p = jnp.exp(s - m_new)
            if mask is not None:
                p = p * mask.astype(jnp.float32)
            alpha = jnp.exp(m_old - m_new)
            l_ref[sl] = alpha * l_ref[sl] + jnp.sum(p, axis=-1, keepdims=True)
            m_ref[sl] = m_new
            ps.append(p.astype(jnp.bfloat16))
            alphas.append(alpha)
        pv = lax.dot_general(jnp.concatenate(ps, axis=0), lat, (((1,), (1,)), ((), ())),
                             preferred_element_type=jnp.float32)
        acc_ref[...] = jnp.concatenate(alphas, axis=0) * acc_ref[...] + pv

    def chunk_body(c, carry):
        slot = c % 2

        @pl.when(c + 1 < n_chunks)
        def _():
            start_chunk(b, c + 1, 1 - slot)

        @pl.when((c + 1 == n_chunks) & (b + 1 < n_seq))
        def _():
            start_chunk(b + 1, 0, 1 - slot)

        wait_chunk(b, c, slot)
        attend(buf_ref[slot], pl.multiple_of(c * lc, lc), None)
        return carry

    lax.fori_loop(0, n_chunks, chunk_body, 0)

    tt = lax.broadcasted_iota(jnp.int32, (DEC_SEQ, PAGE_SIZE), 0)
    jj = lax.broadcasted_iota(jnp.int32, (DEC_SEQ, PAGE_SIZE), 1)
    attend(new_ref[0], n_pages * PAGE_SIZE, jj <= tt)

    o_lat = (acc_ref[...] / jnp.maximum(l_ref[...], 1e-30)).astype(jnp.bfloat16)
    for h in range(MLA_HEADS):
        o_ref[0, h * DEC_SEQ:(h + 1) * DEC_SEQ, :] = jnp.dot(o_lat[h * DEC_SEQ:(h + 1) * DEC_SEQ], wuv_ref[h],
                                                             preferred_element_type=jnp.float32)


def _mla_sample(q_m, rows_new, page_table, cache_mla_t, layer, w_ukv, k_gain, cos, sin):
    DB, S = q_m.shape[:2]
    n_pages = page_table.shape[1]
    assert S == DEC_SEQ and n_pages % MLA_CHUNK_PAGES == 0 and (n_pages // MLA_CHUNK_PAGES) % 2 == 0
    n_pos = n_pages * PAGE_SIZE + PAGE_SIZE
    f32 = jnp.float32
    half = MLA_ROPE // 2
    w = w_ukv.reshape(KV_LORA, MLA_HEADS, MLA_NOPE + MLA_V)
    w_uk = w[:, :, :MLA_NOPE].reshape(KV_LORA, MLA_HEADS * MLA_NOPE).T.astype(jnp.bfloat16)
    w_uv = w[:, :, MLA_NOPE:].transpose(1, 0, 2).astype(jnp.bfloat16)
    qt = q_m.transpose(0, 2, 1, 3)
    eye_h = jnp.eye(MLA_HEADS, dtype=f32)
    q_blk = (qt[:, :, :, None, :MLA_NOPE] * eye_h[None, :, None, :, None]).reshape(
        DB, MLA_ROWS_Q, MLA_HEADS * MLA_NOPE)
    q1, q2 = qt[..., MLA_NOPE:MLA_NOPE + half], qt[..., MLA_NOPE + half:]
    q_pe = jnp.concatenate([q1, q2, q2, -q1], axis=-1).reshape(DB, MLA_ROWS_Q, 2 * MLA_ROPE)
    gn = jnp.tile(k_gain[:MLA_NOPE].astype(f32), MLA_HEADS).reshape(1, MLA_HEADS * MLA_NOPE)
    g1, g2 = k_gain[MLA_NOPE:MLA_NOPE + half].astype(f32), k_gain[MLA_NOPE + half:].astype(f32)
    c, s = cos[:n_pos].T, sin[:n_pos].T
    tbl = jnp.concatenate([c * g1[:, None], c * g2[:, None], s * g1[:, None], s * g2[:, None]], axis=0)
    new_pad = jnp.pad(rows_new.transpose(0, 2, 1), ((0, 0), (0, 0), (0, PAGE_SIZE - S)))
    lc = MLA_CHUNK_PAGES * PAGE_SIZE
    const = lambda shape: pl.BlockSpec(shape, lambda b, pt: (0,) * len(shape))
    out = pl.pallas_call(
        functools.partial(_mla_sample_kernel, layer, n_pages),
        grid_spec=pltpu.PrefetchScalarGridSpec(
            num_scalar_prefetch=1,
            grid=(DB,),
            in_specs=[pl.BlockSpec(memory_space=pl.ANY),
                      pl.BlockSpec((1, MLA_CACHE, PAGE_SIZE), lambda b, pt: (b, 0, 0)),
                      pl.BlockSpec((1, MLA_ROWS_Q, MLA_HEADS * MLA_NOPE), lambda b, pt: (b, 0, 0)),
                      pl.BlockSpec((1, MLA_ROWS_Q, 2 * MLA_ROPE), lambda b, pt: (b, 0, 0)),
                      const((1, MLA_HEADS * MLA_NOPE)),
                      const((MLA_HEADS * MLA_NOPE, KV_LORA)),
                      const((MLA_HEADS, KV_LORA, MLA_V)),
                      const((2 * MLA_ROPE, n_pos))],
            out_specs=pl.BlockSpec((1, MLA_ROWS_Q, MLA_V), lambda b, pt: (b, 0, 0)),
            scratch_shapes=[pltpu.VMEM((2, MLA_CACHE, lc), f32),
                            pltpu.SemaphoreType.DMA((2,)),
                            pltpu.VMEM((MLA_ROWS_Q, 1), f32),
                            pltpu.VMEM((MLA_ROWS_Q, 1), f32),
                            pltpu.VMEM((MLA_ROWS_Q, KV_LORA), f32)]),
        out_shape=jax.ShapeDtypeStruct((DB, MLA_ROWS_Q, MLA_V), f32),
        compiler_params=pltpu.CompilerParams(dimension_semantics=("arbitrary",), vmem_limit_bytes=VMEM_LIMIT_BYTES),
        name="mla_sample_attention",
    )(page_table, cache_mla_t, new_pad, q_blk, q_pe, gn, w_uk, w_uv, tbl)
    return out.reshape(DB, MLA_HEADS, S, MLA_V).transpose(0, 2, 1, 3)


def _nsa_sample_kernel(layer, n_pages, pt_ref, ccmp_ref, csel_ref, q_ref, ncmp_ref, nsel_ref, nwin_ref, wbuf_ref,
                       gain_ref, bcmp_ref, bfar_ref, blast_ref, bnew_ref, bwin_ref, expand_ref, pool_ref,
                       ocmp_ref, osel_ref, owin_ref, buf_ref, sem_ref, sum_ref, msel_ref, m_ref, l_ref, acc_ref):
    b = pl.program_id(0)
    n_seq = pl.num_programs(0)
    cp = NSA_CHUNK_PAGES
    lc = cp * PAGE_SIZE
    n_chunks = n_pages // cp
    bpc = lc // NSA_BLOCK
    n_past = n_pages * PAGE_SIZE
    nbp = n_past // NSA_BLOCK
    nbs = sum_ref.shape[0]
    scale = NSA_DIM ** -0.5

    def page_copy(cache_ref, seq, chunk, p, slot):
        page = pt_ref[seq, chunk * cp + p]
        return pltpu.make_async_copy(cache_ref.at[layer, page], buf_ref.at[slot, :, pl.ds(p * PAGE_SIZE, PAGE_SIZE)],
                                     sem_ref.at[slot])

    def start_chunk(cache_ref, seq, chunk, slot):
        for p in range(cp):
            page_copy(cache_ref, seq, chunk, p, slot).start()

    def wait_chunk(cache_ref, seq, chunk, slot):
        for p in range(cp):
            page_copy(cache_ref, seq, chunk, p, slot).wait()

    @pl.when(b == 0)
    def _():
        start_chunk(ccmp_ref, 0, 0, 0)

    def cmp_body(c, carry):
        slot = c % 2

        @pl.when(c + 1 < n_chunks)
        def _():
            start_chunk(ccmp_ref, b, c + 1, 1 - slot)

        @pl.when(c + 1 == n_chunks)
        def _():
            start_chunk(csel_ref, b, 0, 1 - slot)

        wait_chunk(ccmp_ref, b, c, slot)
        x = buf_ref[slot]
        hi = x.astype(jnp.bfloat16)
        lo = (x - hi.astype(jnp.float32)).astype(jnp.bfloat16)
        nt = (((1,), (1,)), ((), ()))
        sum_ref[pl.ds(pl.multiple_of(c * bpc, bpc), bpc), :] = (
            lax.dot_general(pool_ref[...], hi, nt, preferred_element_type=jnp.float32)
            + lax.dot_general(pool_ref[...], lo, nt, preferred_element_type=jnp.float32))
        return carry

    lax.fori_loop(0, n_chunks, cmp_body, 0)
    pad_rows = nbs - nbp
    new_mean = jnp.sum(ncmp_ref[0], axis=0, keepdims=True) * (1.0 / NSA_BLOCK)
    row0 = lax.broadcasted_iota(jnp.int32, (pad_rows, KV_LANES), 0) == 0
    sum_ref[nbp:, :] = jnp.where(row0, jnp.broadcast_to(new_mean, (pad_rows, KV_LANES)), 0.0)

    summ = sum_ref[...]
    km, vm = summ[:, :LANES], summ[:, LANES:]
    lo_half = lax.broadcasted_iota(jnp.int32, km.shape, 1) < NSA_DIM
    sq = km * km
    ss = jnp.where(lo_half, jnp.sum(jnp.where(lo_half, sq, 0.0), axis=-1, keepdims=True),
                   jnp.sum(jnp.where(lo_half, 0.0, sq), axis=-1, keepdims=True))
    kc = km * lax.rsqrt(ss * (1.0 / NSA_DIM) + EPS) * gain_ref[...]
    vcb = vm.astype(jnp.bfloat16)

    tt = lax.broadcasted_iota(jnp.int32, (ROWS_Q, nbs), 0) % DEC_SEQ
    nn = lax.broadcasted_iota(jnp.int32, (ROWS_Q, nbs), 1)
    cmask = (n_past + tt) >= (nn * NSA_BLOCK + NSA_BLOCK - 1)
    cmaskf = cmask.astype(jnp.float32)
    zq = jnp.zeros((ROWS_Q, NSA_DIM), jnp.float32)
    imps = []
    for g in range(NSA_GROUPS):
        qg = q_ref[0, g]
        q_pad = jnp.concatenate([qg, zq] if g == 0 else [zq, qg], axis=1)
        s = lax.dot_general(q_pad, kc, (((1,), (1,)), ((), ())),
                            precision=lax.Precision.HIGHEST, preferred_element_type=jnp.float32)
        s = jnp.where(cmask, s * scale + bcmp_ref[g], NEG)
        p = jnp.exp(s - jnp.max(s, axis=-1, keepdims=True)) * cmaskf
        pr = p / jnp.maximum(jnp.sum(p, axis=-1, keepdims=True), 1e-30)
        o_full = jnp.dot(pr.astype(jnp.bfloat16), vcb, preferred_element_type=jnp.float32)
        ocmp_ref[0, g] = o_full[:, g * NSA_DIM:(g + 1) * NSA_DIM]
        imp = pr[0:DEC_SEQ]
        for r in range(1, NSA_REP):
            imp = imp + pr[r * DEC_SEQ:(r + 1) * DEC_SEQ]
        imps.append(imp)
    imp = jnp.concatenate(imps, axis=0)
    n16 = lax.broadcasted_iota(jnp.int32, imp.shape, 1)
    t16 = lax.broadcasted_iota(jnp.int32, imp.shape, 0) % DEC_SEQ
    cur = (n_past + t16) // NSA_BLOCK
    forced = (n16 == cur) | (n16 == 0)
    score = jnp.where(n16 <= cur, jnp.where(forced, -NEG, imp), NEG)
    cnt = jnp.zeros(imp.shape, jnp.float32)
    for j in range(nbp + 1):
        cj = score[:, j:j + 1]
        beats = (cj > score) | ((cj == score) & (n16 > j))
        cnt = cnt + beats.astype(jnp.float32)
    msel = (cnt < float(NSA_TOPN)).astype(jnp.float32)
    msel_ref[...] = jnp.zeros(msel_ref.shape, jnp.float32)
    for c in range(n_chunks):
        msel_ref[c, :, 0:bpc] = msel[:, c * bpc:(c + 1) * bpc]
    msel_new = msel[:, nbp:nbp + 1]

    def attend(g, kv_t, bias, mask):
        L = kv_t.shape[1]
        k = kv_t[g * NSA_DIM:(g + 1) * NSA_DIM].astype(jnp.bfloat16)
        v = kv_t[NSA_KV + g * NSA_DIM:NSA_KV + (g + 1) * NSA_DIM].astype(jnp.bfloat16)
        s = jnp.dot(qs[g], k, preferred_element_type=jnp.float32)
        if bias is not None:
            s = s + bias
        s = jnp.where(mask[None], s.reshape(NSA_REP, DEC_SEQ, L), NEG).reshape(ROWS_Q, L)
        m_old = m_ref[g]
        m_new = jnp.maximum(m_old, jnp.max(s, axis=-1, keepdims=True))
        p = jnp.exp(s - m_new)
        alpha = jnp.exp(m_old - m_new)
        l_ref[g] = alpha * l_ref[g] + jnp.sum(p, axis=-1, keepdims=True)
        m_ref[g] = m_new
        acc_ref[g] = alpha * acc_ref[g] + lax.dot_general(p.astype(jnp.bfloat16), v, (((1,), (1,)), ((), ())),
                                                          preferred_element_type=jnp.float32)

    def reset():
        m_ref[...] = jnp.full(m_ref.shape, NEG, jnp.float32)
        l_ref[...] = jnp.zeros(l_ref.shape, jnp.float32)
        acc_ref[...] = jnp.zeros(acc_ref.shape, jnp.float32)

    def result(g):
        return acc_ref[g] / l_ref[g]

    qs = [(q_ref[0, g] * scale).astype(jnp.bfloat16) for g in range(NSA_GROUPS)]
    tn = lax.broadcasted_iota(jnp.int32, (DEC_SEQ, PAGE_SIZE), 0)
    jn = lax.broadcasted_iota(jnp.int32, (DEC_SEQ, PAGE_SIZE), 1)
    new_causal = jn <= tn

    reset()

    def sel_chunk(c, slot, last):
        wait_chunk(csel_ref, b, c, slot)
        kv_t = buf_ref[slot]
        mexp = jnp.dot(msel_ref[c].astype(jnp.bfloat16), expand_ref[...], preferred_element_type=jnp.float32)
        for g in range(NSA_GROUPS):
            attend(g, kv_t, blast_ref[g] if last else None, mexp[g * DEC_SEQ:(g + 1) * DEC_SEQ] > 0.5)

    def sel_body(c, carry):
        slot = c % 2
        start_chunk(csel_ref, b, c + 1, 1 - slot)
        sel_chunk(c, slot, False)
        return carry

    lax.fori_loop(0, n_chunks - 1, sel_body, 0)

    @pl.when(b + 1 < n_seq)
    def _():
        start_chunk(ccmp_ref, b + 1, 0, n_chunks % 2)

    sel_chunk(n_chunks - 1, (n_chunks - 1) % 2, True)
    for g in range(NSA_GROUPS):
        attend(g, nsel_ref[0], bnew_ref[g] - bfar_ref[g], new_causal & (msel_new[g * DEC_SEQ:(g + 1) * DEC_SEQ] > 0.5))
        osel_ref[0, g] = result(g)

    reset()
    n_buf = wbuf_ref.shape[3]
    tw = lax.broadcasted_iota(jnp.int32, (DEC_SEQ, n_buf), 0)
    cw = lax.broadcasted_iota(jnp.int32, (DEC_SEQ, n_buf), 1)
    dist_w = n_buf + tw - cw
    wmask = (dist_w >= 0) & (dist_w < WINDOW)
    for g in range(NSA_GROUPS):
        attend(g, wbuf_ref[0, 0], bwin_ref[g], wmask)
        attend(g, nwin_ref[0], bnew_ref[g], new_causal)
        owin_ref[0, g] = result(g)


def _nsa_sample_tables(rel_bias, n_pages, n_buf):
    n_past = n_pages * PAGE_SIZE
    nbp = n_past // NSA_BLOCK
    nbs = -(-(nbp + 1) // LANES) * LANES
    lc = NSA_CHUNK_PAGES * PAGE_SIZE
    bt = rel_bias.astype(jnp.float32).reshape(N_BUCKETS, NSA_GROUPS, NSA_REP)
    th = _bucket_thresholds()
    t = jnp.arange(DEC_SEQ)

    def table(dist):
        return _bias_of_dist(bt, th, dist).reshape(NSA_GROUPS, ROWS_Q, dist.shape[1])

    bcmp = table(n_past + t[:, None] - (jnp.arange(nbs)[None, :] * NSA_BLOCK + NSA_BLOCK - 1))
    bfar = jnp.broadcast_to(bt[N_BUCKETS - 1][:, :, None, None], (NSA_GROUPS, NSA_REP, DEC_SEQ, 1)).reshape(
        NSA_GROUPS, ROWS_Q, 1)
    blast = table(n_past + t[:, None] - (n_past - lc + jnp.arange(lc))[None, :]) - bfar
    bnew = table(t[:, None] - jnp.arange(PAGE_SIZE)[None, :])
    bwin = table(n_buf + t[:, None] - jnp.arange(n_buf)[None, :])
    expand = (jnp.arange(lc)[None, :] // NSA_BLOCK == jnp.arange(LANES)[:, None]).astype(jnp.bfloat16)
    pool = (expand[:lc // NSA_BLOCK].astype(jnp.float32) * (1.0 / NSA_BLOCK)).astype(jnp.bfloat16)
    return bcmp, bfar, blast, bnew, bwin, expand, pool


def _rows_last(a):
    nd = a.ndim
    perm = tuple(range(nd - 4)) + (nd - 3, nd - 2, nd - 1, nd - 4)
    at = a.transpose(perm)
    return at.reshape(at.shape[:-4] + (KV_LANES, at.shape[-1]))


def _nsa_sample(q_n, kc_new, vc_new, ks_new, vs_new, kw_new, vw_new, page_table, cache_cmp_t, cache_sel_t, win_t,
                layer, kc_gain, tables):
    DB, S = q_n.shape[:2]
    n_pages = page_table.shape[1]
    n_buf = win_t.shape[3]
    lc = NSA_CHUNK_PAGES * PAGE_SIZE
    n_chunks = n_pages // NSA_CHUNK_PAGES
    assert S == DEC_SEQ and n_pages % NSA_CHUNK_PAGES == 0 and n_chunks % 2 == 0 and n_buf == WINDOW
    assert lc + 1 >= MAX_DISTANCE and lc // NSA_BLOCK <= LANES
    assert (n_pages * PAGE_SIZE) % NSA_BLOCK == 0 and S <= NSA_BLOCK
    f32 = jnp.float32
    bcmp, bfar, blast, bnew, bwin, expand, pool = tables
    nbs = bcmp.shape[-1]
    qt = q_n.transpose(0, 2, 3, 1, 4).reshape(DB, NSA_GROUPS, ROWS_Q, NSA_DIM)
    new_t = lambda k, v, width: jnp.pad(jnp.concatenate([k, v], axis=-1).transpose(0, 2, 1),
                                        ((0, 0), (0, 0), (0, width - S)))
    new_cmp = jnp.concatenate([kc_new, vc_new], axis=-1)
    new_sel = new_t(ks_new, vs_new, PAGE_SIZE)
    new_win = new_t(kw_new, vw_new, PAGE_SIZE)
    gain = jnp.tile(kc_gain.astype(f32), NSA_GROUPS).reshape(1, LANES)
    const = lambda shape: pl.BlockSpec(shape, lambda b, pt: (0,) * len(shape))
    per_seq = lambda shape: pl.BlockSpec((1,) + shape, lambda b, pt: (b,) + (0,) * len(shape))
    o_shape = jax.ShapeDtypeStruct((DB, NSA_GROUPS, ROWS_Q, NSA_DIM), f32)
    outs = pl.pallas_call(
        functools.partial(_nsa_sample_kernel, layer, n_pages),
        grid_spec=pltpu.PrefetchScalarGridSpec(
            num_scalar_prefetch=1,
            grid=(DB,),
            in_specs=[pl.BlockSpec(memory_space=pl.ANY), pl.BlockSpec(memory_space=pl.ANY),
                      per_seq((NSA_GROUPS, ROWS_Q, NSA_DIM)),
                      per_seq((S, KV_LANES)), per_seq((KV_LANES, PAGE_SIZE)), per_seq((KV_LANES, PAGE_SIZE)),
                      pl.BlockSpec((1, 1, KV_LANES, n_buf), lambda b, pt: (layer, b, 0, 0)),
                      const((1, LANES)),
                      const((NSA_GROUPS, ROWS_Q, nbs)), const((NSA_GROUPS, ROWS_Q, 1)),
                      const((NSA_GROUPS, ROWS_Q, lc)), const((NSA_GROUPS, ROWS_Q, PAGE_SIZE)),
                      const((NSA_GROUPS, ROWS_Q, n_buf)), const((LANES, lc)), const((lc // NSA_BLOCK, lc))],
            out_specs=[per_seq((NSA_GROUPS, ROWS_Q, NSA_DIM))] * 3,
            scratch_shapes=[pltpu.VMEM((2, KV_LANES, lc), f32),
                            pltpu.SemaphoreType.DMA((2,)),
                            pltpu.VMEM((nbs, KV_LANES), f32),
                            pltpu.VMEM((n_chunks, NSA_GROUPS * DEC_SEQ, LANES), f32),
                            pltpu.VMEM((NSA_GROUPS, ROWS_Q, 1), f32),
                            pltpu.VMEM((NSA_GROUPS, ROWS_Q, 1), f32),
                            pltpu.VMEM((NSA_GROUPS, ROWS_Q, NSA_DIM), f32)]),
        out_shape=[o_shape] * 3,
        compiler_params=pltpu.CompilerParams(dimension_semantics=("arbitrary",), vmem_limit_bytes=VMEM_LIMIT_BYTES),
        name="nsa_sample_attention",
    )(page_table, cache_cmp_t, cache_sel_t, qt, new_cmp, new_sel, new_win, win_t, gain,
      bcmp, bfar, blast, bnew, bwin, expand, pool)
    back = lambda o: o.reshape(DB, NSA_GROUPS, NSA_REP, S, NSA_DIM).transpose(0, 3, 1, 2, 4)
    return tuple(back(o) for o in outs)


def _pool_kernel(pos0, halo_ref, u_ref, w_ref, scale_ref, o_ref, ext_ref):
    i = pl.program_id(1)
    tile = u_ref.shape[1]

    @pl.when(i == 0)
    def _():
        ext_ref[0:POOL_HALO] = halo_ref[0]

    @pl.when(i > 0)
    def _():
        ext_ref[0:POOL_HALO] = ext_ref[tile:tile + POOL_HALO]

    ext_ref[POOL_HALO:POOL_HALO + tile] = u_ref[0]
    pos = pos0 + i * tile + lax.broadcasted_iota(jnp.int32, (tile, 1), 0)
    for gi, w in enumerate(POOL_WINDOWS):
        cols = slice(gi * POOL_GROUP, (gi + 1) * POOL_GROUP)
        x = ext_ref[POOL_HALO:POOL_HALO + tile, cols]
        win_sum = x
        for k in range(1, w):
            win_sum = win_sum + ext_ref[POOL_HALO - k:POOL_HALO - k + tile, cols]
        cnt = jnp.minimum(w, pos + 1).astype(jnp.float32)
        d = win_sum / cnt - x
        y = jnp.dot(d.astype(jnp.bfloat16), w_ref[gi], preferred_element_type=jnp.float32)
        o_ref[0, :, cols] = y * scale_ref[:, cols]


def _pool_mix(u, halo, pos0, w_pool, scale, tile):
    N, L, _ = u.shape
    assert L % tile == 0 and tile % 8 == 0 and (tile >= POOL_HALO or L == tile)
    return pl.pallas_call(
        functools.partial(_pool_kernel, pos0),
        grid=(N, L // tile),
        in_specs=[pl.BlockSpec((1, POOL_HALO, POOL_WIDTH), lambda n, i: (n, 0, 0)),
                  pl.BlockSpec((1, tile, POOL_WIDTH), lambda n, i: (n, i, 0)),
                  pl.BlockSpec((len(POOL_WINDOWS), POOL_GROUP, POOL_GROUP), lambda n, i: (0, 0, 0)),
                  pl.BlockSpec((1, POOL_WIDTH), lambda n, i: (0, 0))],
        out_specs=pl.BlockSpec((1, tile, POOL_WIDTH), lambda n, i: (n, i, 0)),
        out_shape=jax.ShapeDtypeStruct((N, L, POOL_WIDTH), jnp.float32),
        scratch_shapes=[pltpu.VMEM((POOL_HALO + tile, POOL_WIDTH), jnp.float32)],
        compiler_params=pltpu.CompilerParams(
            dimension_semantics=("parallel", "arbitrary"), vmem_limit_bytes=VMEM_LIMIT_BYTES),
        name="pool_mix",
    )(halo, u, w_pool.astype(jnp.bfloat16), scale.astype(jnp.float32).reshape(1, POOL_WIDTH))


def _project(x, pos, p):
    z = _mm(_rmsnorm(x, p['attn_norm']), p['w_in'], tn=384)
    (u_pool, q_lat, kv_lat, k_pe, q_nsa, k_cmp, v_cmp, k_sel, v_sel, k_win, v_win, g_logit) = _split_in(z)
    N, L = x.shape[:2]
    grp = lambda a: a.reshape(N, L, NSA_GROUPS, NSA_DIM)
    q_mla = _mla_q(q_lat, p, pos)
    mla_rows = jnp.concatenate([_rmsnorm(kv_lat, p['mla_kv_norm']), k_pe], axis=-1)
    q_n = _rmsnorm(q_nsa.reshape(N, L, NSA_GROUPS, NSA_REP, NSA_DIM), p['nsa_q_gain'])
    k_sel = _rmsnorm(grp(k_sel), p['nsa_k_gain'][1])
    k_win = _rmsnorm(grp(k_win), p['nsa_k_gain'][2])
    gates = jax.nn.sigmoid(g_logit).reshape(N, L, 3, NSA_GROUPS, NSA_REP)
    return u_pool, q_mla, mla_rows, q_n, grp(k_cmp), grp(v_cmp), k_sel, grp(v_sel), k_win, grp(v_win), gates


def _finish(x, pool_o, mla_o, nsa_o, p):
    N, L = x.shape[:2]
    mix = jnp.concatenate([pool_o, mla_o.reshape(N, L, MLA_WIDTH), nsa_o.reshape(N, L, NSA_WIDTH)], axis=-1)
    x2 = _mm_res(mix.reshape(N * L, MIX_WIDTH), p['w_out'], x.reshape(N * L, D_MODEL))
    h = _rmsnorm(x2, p['ffn_norm'])
    a = _mm_gated(h, p['w_gate'], p['w_up'])
    return _mm_res(a, p['w_down'], x2).reshape(N, L, D_MODEL)


def kernel(x_prompt, x_sample, cache_mla, cache_nsa_cmp, cache_nsa_sel, state_nsa_win, state_pool, page_table,
           rel_bias, attn_norm, w_in, pool_w, pool_scale, mla_q_norm, mla_kv_norm, w_uq, w_ukv,
           mla_q_gain, mla_k_gain, nsa_q_gain, nsa_k_gain, w_out, ffn_norm, w_gate, w_up, w_down):
    xp, xs = x_prompt, x_sample
    B, T = xp.shape[:2]
    DB, S = xs.shape[:2]
    n_pages = page_table.shape[1]
    n_buf = state_nsa_win.shape[2]
    n_ctx = state_pool.shape[2]
    pos_p = jnp.arange(T)
    pos_s = PAST_LEN + jnp.arange(S)
    cos, sin = _rope_tables(n_pages * PAGE_SIZE + PAGE_SIZE)
    prompt_tables = _nsa_prompt_tables(rel_bias, T)
    sample_tables = _nsa_sample_tables(rel_bias, n_pages, n_buf)
    cache_mla_t = cache_mla.transpose(0, 1, 3, 2)
    cache_cmp_t = _rows_last(cache_nsa_cmp)
    cache_sel_t = _rows_last(cache_nsa_sel)
    win_state_t = _rows_last(state_nsa_win)
    pool_halo_p = jnp.zeros((B, POOL_HALO, POOL_WIDTH), jnp.float32)
    pool_halo_s = jnp.pad(state_pool, ((0, 0), (0, 0), (POOL_HALO - n_ctx, 0), (0, 0)))
    flat = lambda a: a.reshape(a.shape[:2] + (NSA_KV,))
    nmla_p, ncmp_p, nsel_p, nwin_p, npool_p = [], [], [], [], []
    nmla_s, ncmp_s, nsel_s, nwin_s, npool_s = [], [], [], [], []
    for l in range(DEPTH):
        p = {'attn_norm': attn_norm[l], 'w_in': w_in[l], 'mla_q_norm': mla_q_norm[l],
             'mla_kv_norm': mla_kv_norm[l], 'w_uq': w_uq[l], 'w_ukv': w_ukv[l],
             'mla_q_gain': mla_q_gain[l], 'mla_k_gain': mla_k_gain[l], 'nsa_q_gain': nsa_q_gain[l],
             'nsa_k_gain': nsa_k_gain[l], 'w_out': w_out[l], 'ffn_norm': ffn_norm[l],
             'w_gate': w_gate[l], 'w_up': w_up[l], 'w_down': w_down[l]}

        u, q_m, rows_m, q_n, kc, vc, ks, vs, kw, vw, gates = _project(xp, pos_p, p)
        pool_o = _pool_mix(u, pool_halo_p, 0, pool_w[l], pool_scale[l], POOL_TILE)
        k_m, v_m = _mla_kv(rows_m, p, pos_p)
        mla_o = _mla_prompt(q_m, k_m, v_m)
        o_cmp, o_sel, o_win = _nsa_prompt(q_n, flat(kc), flat(vc), flat(ks), flat(vs), flat(kw), flat(vw),
                                          p['nsa_k_gain'][0], prompt_tables)
        nsa_o = _nsa_combine(gates, o_cmp, o_sel, o_win)
        xp = _finish(xp, pool_o, mla_o, nsa_o, p)
        nmla_p.append(rows_m)
        ncmp_p.append(jnp.stack([kc, vc], axis=2))
        nsel_p.append(jnp.stack([ks, vs], axis=2))
        nwin_p.append(jnp.stack([kw, vw], axis=2)[:, T - min(WINDOW, T):])
        npool_p.append(u[:, T - POOL_STATE:])

        u, q_m, rows_m, q_n, kc, vc, ks, vs, kw, vw, gates = _project(xs, pos_s, p)
        u_ext = jnp.concatenate([state_pool[l], u], axis=1)
        pool_o = _pool_mix(u, pool_halo_s[l], PAST_LEN, pool_w[l], pool_scale[l], S)
        mla_o = _mla_sample(q_m, rows_m, page_table, cache_mla_t, l, p['w_ukv'], p['mla_k_gain'], cos, sin)
        o_cmp, o_sel, o_win = _nsa_sample(q_n, flat(kc), flat(vc), flat(ks), flat(vs), flat(kw), flat(vw),
                                          page_table, cache_cmp_t, cache_sel_t, win_state_t, l,
                                          p['nsa_k_gain'][0], sample_tables)
        nsa_o = _nsa_combine(gates, o_cmp, o_sel, o_win)
        xs = _finish(xs, pool_o, mla_o, nsa_o, p)
        nmla_s.append(rows_m)
        ncmp_s.append(jnp.stack([kc, vc], axis=2))
        nsel_s.append(jnp.stack([ks, vs], axis=2))
        nwin_s.append(jnp.concatenate([state_nsa_win[l], jnp.stack([kw, vw], axis=2)], axis=1)[:, S:S + n_buf])
        npool_s.append(u_ext[:, S:S + n_ctx])

    return (xp, xs,
            jnp.stack(nmla_p), jnp.stack(ncmp_p), jnp.stack(nsel_p), jnp.stack(nwin_p), jnp.stack(npool_p),
            jnp.stack(nmla_s), jnp.stack(ncmp_s), jnp.stack(nsel_s), jnp.stack(nwin_s), jnp.stack(npool_s))
```
